```python
import math
import jax, jax.numpy as jnp
from jax import lax
import numpy as np

D_MODEL = 1024
BATCH = 32
SEQ = 2048
DEPTH = 4

N_MEM = 256
HEAD_DIM = 64
SWA_HEADS = 8
SWA_KV_HEADS = 2
SWA_GROUP = SWA_HEADS // SWA_KV_HEADS
WINDOW = 128
SB_HEADS = 8
SB_BLOCK = 128
HGRN_HEAD_DIM = 128
HGRN_HEADS = D_MODEL // HGRN_HEAD_DIM
HGRN_CHUNK = 32
XA_HEADS = 4
XA_HEAD_DIM = D_MODEL // XA_HEADS
D_FF = 128 * ((8 * D_MODEL // 3 + 127) // 128)
CONV_WIDTH = 3
ROPE_THETA = 10000.0
EPS = 1e-6
N_EVEN = (DEPTH + 1) // 2
N_ODD = DEPTH // 2
SWA_Q_W = SWA_HEADS * HEAD_DIM
SWA_KV_W = SWA_KV_HEADS * HEAD_DIM
SB_W = SB_HEADS * HEAD_DIM
AB_IN = SWA_Q_W + 2 * SWA_KV_W + 3 * SB_W
AB_OUT = SWA_Q_W + SB_W

kernel_name = "hybrid_swa_stickbreak_hgrn2_trunk"


def rms_norm(x, gain):
    x32 = x.astype(jnp.float32)
    y = x32 * lax.rsqrt(jnp.mean(x32 * x32, axis=-1, keepdims=True) + EPS)
    return (y * gain.astype(jnp.float32)).astype(x.dtype)


def rope(x, positions):
    d = x.shape[-1]
    inv_freq = ROPE_THETA ** (-jnp.arange(0, d, 2, dtype=jnp.float32) / d)
    ang = positions.astype(jnp.float32)[..., None] * inv_freq
    cos, sin = jnp.cos(ang)[:, :, None, :], jnp.sin(ang)[:, :, None, :]
    x1, x2 = x[..., : d // 2], x[..., d // 2:]
    return jnp.concatenate([x1 * cos - x2 * sin, x2 * cos + x1 * sin], axis=-1)


def sliding_window_attention(q, k, v, sinks):
    B, S = q.shape[:2]
    nb = S // WINDOW
    qb = q.reshape(B, nb, WINDOW, SWA_KV_HEADS, SWA_GROUP, HEAD_DIM)

    def band(t):
        prev = jnp.pad(t, ((0, 0), (WINDOW, 0), (0, 0), (0, 0)))[:, :S]
        return jnp.concatenate([prev.reshape(B, nb, WINDOW, SWA_KV_HEADS, HEAD_DIM),
                                t.reshape(B, nb, WINDOW, SWA_KV_HEADS, HEAD_DIM)], axis=2)

    kb, vb = band(k), band(v)
    s = jnp.einsum('bnqhgd,bnkhd->bnhgqk', qb, kb) * (HEAD_DIM ** -0.5)
    i = jnp.arange(WINDOW)[:, None]
    j = jnp.arange(2 * WINDOW)[None, :]
    blk = jnp.arange(nb)[:, None, None]
    mask = (j > i) & (j <= i + WINDOW) & (blk * WINDOW - WINDOW + j >= 0)
    s = jnp.where(mask[None, :, None, None], s, -jnp.inf)
    sink = sinks.astype(jnp.float32).reshape(SWA_KV_HEADS, SWA_GROUP)[None, None, :, :, None, None]
    m = jnp.maximum(jnp.max(s, axis=-1, keepdims=True), sink)
    p = jnp.exp(s - m)
    p = p / (jnp.sum(p, axis=-1, keepdims=True) + jnp.exp(sink - m))
    o = jnp.einsum('bnhgqk,bnkhd->bnqhgd', p, vb)
    return o.reshape(B, S, SWA_Q_W)


def stick_breaking_attention(q, k, v):
    B, S = q.shape[:2]
    outs = []
    for n in range(S // SB_BLOCK):
        t0, t1 = n * SB_BLOCK, (n + 1) * SB_BLOCK
        z = jnp.einsum('bqhd,bkhd->bhqk', q[:, t0:t1], k[:, :t1]) * (HEAD_DIM ** -0.5)
        causal = jnp.arange(t1)[None, :] < (t0 + jnp.arange(SB_BLOCK))[:, None]
        log_fail = jnp.where(causal, -jax.nn.softplus(z), 0.0)
        log_after = lax.cumsum(log_fail, axis=3, reverse=True) - log_fail
        a = jnp.where(causal, jnp.exp(jax.nn.log_sigmoid(z) + log_after), 0.0)
        outs.append(jnp.einsum('bhqk,bkhd->bqhd', a, v[:, :t1]))
    return jnp.concatenate(outs, axis=1).reshape(B, S, SB_W)


def hgrn2_chunkwise(q, k, v, log_f):
    B, S, H, dk = q.shape
    dv = v.shape[-1]
    nc = S // HGRN_CHUNK

    def chunks(t):
        return t.reshape(B, nc, HGRN_CHUNK, H, t.shape[-1]).transpose(1, 0, 3, 2, 4)

    qc, kc, vc = chunks(q), chunks(k), chunks(v)
    bc = jnp.cumsum(chunks(log_f), axis=3)
    tril = jnp.tril(jnp.ones((HGRN_CHUNK, HGRN_CHUNK), dtype=bool))

    def step(state, inp):
        qi, ki, vi, bi = inp
        b_last = bi[:, :, -1:, :]
        rel = jnp.where(tril[None, None, :, :, None],
                        bi[:, :, :, None, :] - bi[:, :, None, :, :], -jnp.inf)
        scores = jnp.einsum('bhtd,bhsd,bhtsd->bhts', qi, ki, jnp.exp(rel))
        o = (jnp.einsum('bhts,bhse->bhte', scores, vi)
             + jnp.einsum('bhtd,bhde->bhte', qi * jnp.exp(bi), state))
        state = (jnp.exp(b_last)[:, :, 0, :, None] * state
                 + jnp.einsum('bhsd,bhse->bhde', ki * jnp.exp(b_last - bi), vi))
        return state, o

    s0 = jnp.zeros((B, H, dk, dv), jnp.float32)
    _, o = lax.scan(step, s0, (qc, kc, vc, bc))
    return o.transpose(1, 0, 3, 2, 4).reshape(B, S, H, dv)


def swa_sb_mixer(h, positions, w_in, w_out, q_norm, k_norm, sinks):
    B, S, _ = h.shape
    f32 = jnp.float32
    cuts = (SWA_Q_W, SWA_Q_W + SWA_KV_W, SWA_Q_W + 2 * SWA_KV_W,
            SWA_Q_W + 2 * SWA_KV_W + SB_W, SWA_Q_W + 2 * SWA_KV_W + 2 * SB_W)
    qa, ka, va, qb, kb, vb = jnp.split((h @ w_in).astype(f32), cuts, axis=-1)
    qa = rope(rms_norm(qa.reshape(B, S, SWA_HEADS, HEAD_DIM), q_norm), positions)
    ka = rope(rms_norm(ka.reshape(B, S, SWA_KV_HEADS, HEAD_DIM), k_norm), positions)
    va = va.reshape(B, S, SWA_KV_HEADS, HEAD_DIM)
    out_a = sliding_window_attention(qa, ka, va, sinks)
    heads_b = lambda t: t.reshape(B, S, SB_HEADS, HEAD_DIM)
    out_b = stick_breaking_attention(heads_b(qb), heads_b(kb), heads_b(vb))
    return jnp.concatenate([out_a, out_b], axis=-1).astype(h.dtype) @ w_out


def hgrn2_mixer(h, w_in, w_out, o_norm, lower_bound):
    B, S, _ = h.shape
    f32 = jnp.float32
    q, f_logit, i_in, g = jnp.split((h @ w_in).astype(f32), 4, axis=-1)
    lb = lower_bound.astype(f32)
    log_f = jnp.log(lb + (1.0 - lb) * jax.nn.sigmoid(f_logit))
    k = (1.0 - lb) * jax.nn.sigmoid(-f_logit)
    heads = lambda t: t.reshape(B, S, HGRN_HEADS, HGRN_HEAD_DIM)
    o = hgrn2_chunkwise(heads(q), heads(k), heads(i_in), heads(log_f))
    o = rms_norm(o, o_norm) * jax.nn.silu(heads(g))
    return o.reshape(B, S, HGRN_HEADS * HGRN_HEAD_DIM).astype(h.dtype) @ w_out


def memory_cross_attention(h, mem_n, w_q, w_kv, w_o, q_norm, k_norm):
    B, S, _ = h.shape
    M = mem_n.shape[1]
    f32 = jnp.float32
    q = rms_norm((h @ w_q).astype(f32).reshape(B, S, XA_HEADS, XA_HEAD_DIM), q_norm)
    kv = (mem_n @ w_kv).astype(f32).reshape(B, M, 2, XA_HEADS, XA_HEAD_DIM)
    k = rms_norm(kv[:, :, 0], k_norm)
    v = kv[:, :, 1]
    s = jnp.einsum('bshd,bmhd->bhsm', q, k) * (XA_HEAD_DIM ** -0.5)
    p = jax.nn.softmax(s, axis=-1)
    o = jnp.einsum('bhsm,bmhd->bshd', p, v).reshape(B, S, D_MODEL)
    return o.astype(h.dtype) @ w_o


def conv_ffn(h, w_up, conv_w, conv_b, w_down):
    S = h.shape[1]
    u = h @ w_up
    shift = lambda t, n: jnp.pad(t, ((0, 0), (n, 0), (0, 0)))[:, :S]
    u = sum(conv_w[CONV_WIDTH - 1 - n] * shift(u, n) for n in range(CONV_WIDTH)) + conv_b
    gate, up = jnp.split(u, 2, axis=-1)
    return (jax.nn.silu(gate) * up) @ w_down


def setup_inputs(seed: int = 0) -> dict:
    key = jax.random.key(seed)
    ks = jax.random.split(key, 24)
    f32 = jnp.float32

    def dense(k, shape):
        return jax.random.normal(k, shape, f32) * (shape[-2] ** -0.5)

    def gain(k, shape):
        return 1.0 + 0.02 * jax.random.normal(k, shape, f32)

    x = jax.random.normal(ks[0], (BATCH, SEQ, D_MODEL), f32)
    mem = jax.random.normal(ks[1], (BATCH, N_MEM, D_MODEL), f32)
    positions = (jax.random.randint(ks[2], (BATCH, 1), 0, 1024, dtype=jnp.int32)
                 + jnp.arange(SEQ, dtype=jnp.int32)[None, :])
    return {
        "x": x,
        "mem": mem,
        "positions": positions,
        "norm_mix": gain(ks[3], (DEPTH, D_MODEL)),
        "norm_cross": gain(ks[4], (DEPTH, D_MODEL)),
        "norm_mem": gain(ks[5], (DEPTH, D_MODEL)),
        "norm_ffn": gain(ks[6], (DEPTH, D_MODEL)),
        "ab_w_in": dense(ks[7], (N_EVEN, D_MODEL, AB_IN)),
        "ab_w_out": dense(ks[8], (N_EVEN, AB_OUT, D_MODEL)),
        "swa_q_norm": gain(ks[9], (N_EVEN, HEAD_DIM)),
        "swa_k_norm": gain(ks[10], (N_EVEN, HEAD_DIM)),
        "swa_sinks": 0.5 * jax.random.normal(ks[11], (N_EVEN, SWA_HEADS), f32),
        "hgrn_w_in": dense(ks[12], (N_ODD, D_MODEL, 4 * HGRN_HEADS * HGRN_HEAD_DIM)),
        "hgrn_w_out": dense(ks[13], (N_ODD, HGRN_HEADS * HGRN_HEAD_DIM, D_MODEL)),
        "hgrn_o_norm": gain(ks[14], (N_ODD, HGRN_HEAD_DIM)),
        "hgrn_lb": jax.random.normal(ks[15], (DEPTH, HGRN_HEADS * HGRN_HEAD_DIM), f32),
        "xa_w_q": dense(ks[16], (DEPTH, D_MODEL, D_MODEL)),
        "xa_w_kv": dense(ks[17], (DEPTH, D_MODEL, 2 * D_MODEL)),
        "xa_w_o": dense(ks[18], (DEPTH, D_MODEL, D_MODEL)),
        "xa_q_norm": gain(ks[19], (DEPTH, XA_HEAD_DIM)),
        "xa_k_norm": gain(ks[20], (DEPTH, XA_HEAD_DIM)),
        "ffn_w_up": dense(ks[21], (DEPTH, D_MODEL, 2 * D_FF)),
        "ffn_conv_w": jax.random.normal(ks[22], (DEPTH, CONV_WIDTH, 2 * D_FF), f32) * (CONV_WIDTH ** -0.5),
        "ffn_conv_b": 0.02 * jax.random.normal(jax.random.fold_in(ks[22], 1), (DEPTH, 2 * D_FF), f32),
        "ffn_w_down": dense(ks[23], (DEPTH, D_FF, D_MODEL)),
    }


def reference(x, mem, positions, norm_mix, norm_cross, norm_mem, norm_ffn,
              ab_w_in, ab_w_out, swa_q_norm, swa_k_norm, swa_sinks,
              hgrn_w_in, hgrn_w_out, hgrn_o_norm, hgrn_lb,
              xa_w_q, xa_w_kv, xa_w_o, xa_q_norm, xa_k_norm,
              ffn_w_up, ffn_conv_w, ffn_conv_b, ffn_w_down):
    p_lb = jax.nn.softmax(hgrn_lb.astype(jnp.float32), axis=0)
    lower_bounds = jnp.cumsum(p_lb, axis=0) - p_lb[0]
    for l in range(DEPTH):
        h = rms_norm(x, norm_mix[l])
        if l % 2 == 0:
            e = l // 2
            x = x + swa_sb_mixer(h, positions, ab_w_in[e], ab_w_out[e],
                                 swa_q_norm[e], swa_k_norm[e], swa_sinks[e])
        else:
            o = l // 2
            x = x + hgrn2_mixer(h, hgrn_w_in[o], hgrn_w_out[o], hgrn_o_norm[o], lower_bounds[l])
        mem_n = rms_norm(mem, norm_mem[l])
        x = x + memory_cross_attention(rms_norm(x, norm_cross[l]), mem_n, xa_w_q[l], xa_w_kv[l],
                                       xa_w_o[l], xa_q_norm[l], xa_k_norm[l])
        x = x + conv_ffn(rms_norm(x, norm_ffn[l]), ffn_w_up[l], ffn_conv_w[l], ffn_conv_b[l], ffn_w_down[l])
    return x
```

```python
import functools

import jax
import jax.numpy as jnp
from jax import lax
from jax.experimental import pallas as pl
from jax.experimental.pallas import tpu as pltpu

F32 = jnp.float32
BF16 = jnp.bfloat16

D_MODEL = 1024
HEAD_DIM = 64
SWA_HEADS = 8
SWA_KV_HEADS = 2
WINDOW = 128
SB_HEADS = 8
SB_BLOCK = 128
HGRN_HEAD_DIM = 128
HGRN_HEADS = D_MODEL // HGRN_HEAD_DIM
XA_HEADS = 4
XA_HEAD_DIM = D_MODEL // XA_HEADS
ROPE_THETA = 10000.0
EPS = 1e-6
SWA_Q_W = SWA_HEADS * HEAD_DIM
SWA_KV_W = SWA_KV_HEADS * HEAD_DIM
SB_W = SB_HEADS * HEAD_DIM
AB_IN = SWA_Q_W + 2 * SWA_KV_W + 3 * SB_W

LANES = 128
MXU_DIM = 256
TOKEN_TILE = 512
FF_CHUNK = 256
HGRN_CHUNK = 32
HGRN_PRE_TILE = 256
HGRN_HEADS_PER_STEP = 4
HGRN_SAFE_LOG_DECAY = -60.0
SB_Q_TILE = 512
VMEM_LIMIT = 56 * 1024 * 1024

NT_DIMS = (((1,), (1,)), ((), ()))
TN_DIMS = (((0,), (0,)), ((), ()))


def _params(*sem):
    return pltpu.CompilerParams(dimension_semantics=sem, vmem_limit_bytes=VMEM_LIMIT)


def _resident(shape):
    zeros = (0,) * len(shape)
    return pl.BlockSpec(shape, lambda *_: zeros, pipeline_mode=pl.Buffered(1))


def _rms(x, gain):
    ms = jnp.mean(x * x, axis=-1, keepdims=True)
    return x * lax.rsqrt(ms + EPS) * gain


def _split_bf16(x):
    hi = x.astype(BF16)
    lo = (x - hi.astype(F32)).astype(BF16)
    return hi, lo


def _sigmoid_pair(x):
    e = jnp.exp(-jnp.abs(x))
    big = 1.0 / (1.0 + e)
    small = e * big
    pos = x >= 0
    return jnp.where(pos, big, small), jnp.where(pos, small, big)


def _norm_matmul_kernel(x_ref, g_ref, w_ref, o_ref, h_ref):
    @pl.when(pl.program_id(1) == 0)
    def _():
        h_ref[...] = _rms(x_ref[...], g_ref[...]).astype(BF16)

    o_ref[...] = jnp.dot(h_ref[...], w_ref[...], preferred_element_type=F32).astype(o_ref.dtype)


def _norm_matmul(x, gain, w, tn):
    m, d = x.shape
    n = w.shape[1]
    tm = TOKEN_TILE
    return pl.pallas_call(
        _norm_matmul_kernel,
        grid=(m // tm, n // tn),
        in_specs=[
            pl.BlockSpec((tm, d), lambda i, j: (i, 0)),
            pl.BlockSpec((1, d), lambda i, j: (0, 0)),
            pl.BlockSpec((d, tn), lambda i, j: (0, j)),
        ],
        out_specs=pl.BlockSpec((tm, tn), lambda i, j: (i, j)),
        out_shape=jax.ShapeDtypeStruct((m, n), BF16),
        scratch_shapes=[pltpu.VMEM((tm, d), BF16)],
        compiler_params=_params("arbitrary", "arbitrary"),
    )(x, gain.reshape(1, d), w)


def _proj_residual_kernel(n_in, x_ref, *refs):
    a_refs, w_refs, o_ref = refs[:n_in], refs[n_in:2 * n_in], refs[2 * n_in]
    acc = x_ref[...]
    for a_ref, w_ref in zip(a_refs, w_refs):
        acc = acc + jnp.dot(a_ref[...], w_ref[...], preferred_element_type=F32)
    o_ref[...] = acc


def _proj_residual(x, acts, weights):
    m, d = x.shape
    tm = TOKEN_TILE
    row = lambda i: (i, 0)
    return pl.pallas_call(
        functools.partial(_proj_residual_kernel, len(acts)),
        grid=(m // tm,),
        in_specs=[pl.BlockSpec((tm, d), row)]
        + [pl.BlockSpec((tm, a.shape[1]), row) for a in acts]
        + [_resident(w.shape) for w in weights],
        out_specs=pl.BlockSpec((tm, d), row),
        out_shape=jax.ShapeDtypeStruct((m, d), F32),
        compiler_params=_params("arbitrary"),
    )(x, *acts, *weights)


def _rope_table_kernel(pos_ref, inv_ref, cos_ref, sin_ref):
    ang = pos_ref[...] * inv_ref[...]
    lane = lax.broadcasted_iota(jnp.int32, ang.shape, 1)
    first_half = (lane % HEAD_DIM) < HEAD_DIM // 2
    cos_ref[...] = jnp.cos(ang)
    sin = jnp.sin(ang)
    sin_ref[...] = jnp.where(first_half, -sin, sin)


def _rope_tables(positions):
    m = positions.size
    tm = TOKEN_TILE
    half = HEAD_DIM // 2
    inv_freq = ROPE_THETA ** (-jnp.arange(0, HEAD_DIM, 2, dtype=F32) / HEAD_DIM)
    inv = jnp.tile(inv_freq, LANES // half).reshape(1, LANES)
    pos = positions.astype(F32).reshape(m, 1)
    row = lambda i: (i, 0)
    return pl.pallas_call(
        _rope_table_kernel,
        grid=(m // tm,),
        in_specs=[pl.BlockSpec((tm, 1), row), pl.BlockSpec((1, LANES), lambda i: (0, 0))],
        out_specs=[pl.BlockSpec((tm, LANES), row), pl.BlockSpec((tm, LANES), row)],
        out_shape=[jax.ShapeDtypeStruct((m, LANES), F32)] * 2,
        compiler_params=_params("arbitrary"),
    )(pos, inv)


def _group_ones(width):
    r = lax.broadcasted_iota(jnp.int32, (width, width), 0) // HEAD_DIM
    c = lax.broadcasted_iota(jnp.int32, (width, width), 1) // HEAD_DIM
    return (r == c).astype(BF16)


def _head_norm_rope(x, gain, cos, sin):
    hi, lo = _split_bf16(x * x)
    ones = _group_ones(LANES)
    ss = (jnp.dot(hi, ones, preferred_element_type=F32)
          + jnp.dot(lo, ones, preferred_element_type=F32))
    y = x * lax.rsqrt(ss * (1.0 / HEAD_DIM) + EPS) * gain
    lane = lax.broadcasted_iota(jnp.int32, y.shape, 1)
    first_half = (lane % HEAD_DIM) < HEAD_DIM // 2
    partner = jnp.where(first_half,
                        pltpu.roll(y, LANES - HEAD_DIM // 2, 1),
                        pltpu.roll(y, HEAD_DIM // 2, 1))
    return y * cos + partner * sin


def _swa_kernel(seq, qkv_ref, cos_ref, sin_ref, qg_ref, kg_ref, sink_ref, o_ref,
                q_scr, k_scr, v_scr):
    rt = min(seq, TOKEN_TILE)
    n_q_chunks = SWA_Q_W // LANES
    for r in range(seq // rt):
        rows = slice(r * rt, (r + 1) * rt)
        cos, sin = cos_ref[rows, :], sin_ref[rows, :]
        for c in range(n_q_chunks):
            cols = slice(c * LANES, (c + 1) * LANES)
            xq = qkv_ref[rows, cols].astype(F32)
            q = _head_norm_rope(xq, qg_ref[...], cos, sin)
            q_scr[c, rows, :] = (q * HEAD_DIM ** -0.5).astype(BF16)
        xk = qkv_ref[rows, SWA_Q_W:SWA_Q_W + SWA_KV_W].astype(F32)
        k = _head_norm_rope(xk, kg_ref[...], cos, sin)
        v = qkv_ref[rows, SWA_Q_W + SWA_KV_W:SWA_Q_W + 2 * SWA_KV_W].astype(F32)
        low = lax.broadcasted_iota(jnp.int32, k.shape, 1) < HEAD_DIM
        for t, scr in ((k, k_scr), (v, v_scr)):
            swapped = pltpu.roll(t, HEAD_DIM, 1)
            scr[0, 0, rows, :] = jnp.where(low, t, 0.0).astype(BF16)
            scr[1, 0, rows, :] = jnp.where(low, 0.0, swapped).astype(BF16)
            scr[0, 1, rows, :] = jnp.where(low, swapped, 0.0).astype(BF16)
            scr[1, 1, rows, :] = jnp.where(low, 0.0, t).astype(BF16)

    w = WINDOW
    qi = lax.broadcasted_iota(jnp.int32, (2 * w, w), 0) % w
    kj = lax.broadcasted_iota(jnp.int32, (2 * w, w), 1)
    top = lax.broadcasted_iota(jnp.int32, (2 * w, 1), 0) < w
    cur_mask = kj <= qi

    def block(n, carry):
        r0 = pl.multiple_of(n * w, w)
        p0 = pl.multiple_of(jnp.maximum(n - 1, 0) * w, w)
        prev_mask = (kj > qi) & (n > 0)
        for g in range(SWA_KV_HEADS):
            qa = jnp.concatenate([q_scr[2 * g, pl.ds(r0, w), :],
                                  q_scr[2 * g + 1, pl.ds(r0, w), :]], axis=0)
            out = jnp.zeros((2 * w, LANES), F32)
            for slot in range(2):
                s_cur = lax.dot_general(qa, k_scr[slot, g, pl.ds(r0, w), :], NT_DIMS,
                                        preferred_element_type=F32)
                s_prev = lax.dot_general(qa, k_scr[slot, g, pl.ds(p0, w), :], NT_DIMS,
                                         preferred_element_type=F32)
                s_cur = jnp.where(cur_mask, s_cur, -jnp.inf)
                s_prev = jnp.where(prev_mask, s_prev, -jnp.inf)
                sink = jnp.where(top, sink_ref[4 * g + slot], sink_ref[4 * g + 2 + slot])
                m = jnp.maximum(jnp.maximum(jnp.max(s_cur, axis=-1, keepdims=True),
                                            jnp.max(s_prev, axis=-1, keepdims=True)), sink)
                p_cur = jnp.exp(s_cur - m)
                p_prev = jnp.exp(s_prev - m)
                den = (jnp.sum(p_cur, axis=-1, keepdims=True)
                       + jnp.sum(p_prev, axis=-1, keepdims=True) + jnp.exp(sink - m))
                inv = 1.0 / den
                out = out + jnp.dot((p_cur * inv).astype(BF16), v_scr[slot, g, pl.ds(r0, w), :],
                                    preferred_element_type=F32)
                out = out + jnp.dot((p_prev * inv).astype(BF16), v_scr[slot, g, pl.ds(p0, w), :],
                                    preferred_element_type=F32)
            o_ref[pl.ds(r0, w), 2 * g * LANES:(2 * g + 1) * LANES] = out[:w].astype(BF16)
            o_ref[pl.ds(r0, w), (2 * g + 1) * LANES:(2 * g + 2) * LANES] = out[w:].astype(BF16)
        return carry

    lax.fori_loop(0, seq // w, block, 0)


def _swa_attention(qkv, cos, sin, q_gain, k_gain, sinks, batch, seq):
    width = SWA_Q_W + 2 * SWA_KV_W
    assert AB_IN % width == 0
    const = lambda b: (0, 0)
    return pl.pallas_call(
        functools.partial(_swa_kernel, seq),
        grid=(batch,),
        in_specs=[
            pl.BlockSpec((seq, width), lambda b: (b, 0)),
            pl.BlockSpec((seq, LANES), lambda b: (b, 0)),
            pl.BlockSpec((seq, LANES), lambda b: (b, 0)),
            pl.BlockSpec((1, LANES), const),
            pl.BlockSpec((1, LANES), const),
            pl.BlockSpec(memory_space=pltpu.SMEM),
        ],
        out_specs=pl.BlockSpec((seq, SWA_Q_W), lambda b: (b, 0)),
        out_shape=jax.ShapeDtypeStruct((batch * seq, SWA_Q_W), BF16),
        scratch_shapes=[
            pltpu.VMEM((SWA_Q_W // LANES, seq, LANES), BF16),
            pltpu.VMEM((2, SWA_KV_HEADS, seq, LANES), BF16),
            pltpu.VMEM((2, SWA_KV_HEADS, seq, LANES), BF16),
        ],
        compiler_params=_params("arbitrary"),
    )(qkv, cos, sin,
      jnp.tile(q_gain, LANES // HEAD_DIM).reshape(1, LANES),
      jnp.tile(k_gain, LANES // HEAD_DIM).reshape(1, LANES),
      sinks)


def _sb_kernel(seq, q_ref, k_ref, v_ref, o_ref, k_scr, v_scr, acc_ref, carry_ref):
    blk = SB_BLOCK
    tq = min(seq, SB_Q_TILE)
    low = lax.broadcasted_iota(jnp.int32, (seq, LANES), 1) < HEAD_DIM
    kf, vf = k_ref[...], v_ref[...]
    zero = jnp.zeros_like(kf)
    k_scr[0] = jnp.where(low, kf, zero)
    k_scr[1] = jnp.where(low, zero, kf)
    v_scr[0] = jnp.where(low, vf, zero)
    v_scr[1] = jnp.where(low, zero, vf)

    j = lax.broadcasted_iota(jnp.int32, (2 * blk, blk), 0) % blk
    s = lax.broadcasted_iota(jnp.int32, (2 * blk, blk), 1)
    later = (j > s).astype(BF16)
    key_lane = lax.broadcasted_iota(jnp.int32, (tq, blk), 1)
    q_row = lax.broadcasted_iota(jnp.int32, (tq, blk), 0)

    for qi in range(seq // tq):
        rows = slice(qi * tq, (qi + 1) * tq)
        qt = q_ref[rows, :] * jnp.asarray(HEAD_DIM ** -0.5, BF16)
        acc_ref[...] = jnp.zeros_like(acc_ref)
        carry_ref[...] = jnp.zeros_like(carry_ref)
        n_kb = (qi + 1) * tq // blk

        def step(i, c, qt=qt, n_kb=n_kb, qi=qi):
            kb = n_kb - 1 - i
            k0 = pl.multiple_of(kb * blk, blk)
            valid = (k0 + key_lane) < (qi * tq + q_row)
            pv = jnp.zeros((tq, LANES), F32)
            for h in range(2):
                z = lax.dot_general(qt, k_scr[h, pl.ds(k0, blk), :], NT_DIMS,
                                    preferred_element_type=F32)
                softplus = jnp.maximum(z, 0.0) + jnp.log(1.0 + jnp.exp(-jnp.abs(z)))
                log_fail = jnp.where(valid, -softplus, 0.0)
                hi, lo = _split_bf16(log_fail)
                after = jnp.dot(jnp.concatenate([hi, lo], axis=1), later,
                                preferred_element_type=F32)
                carry = carry_ref[h]
                a = jnp.where(valid, jnp.exp(z + log_fail + after + carry), 0.0)
                pv = pv + jnp.dot(a.astype(BF16), v_scr[h, pl.ds(k0, blk), :],
                                  preferred_element_type=F32)
                carry_ref[h] = carry + (after[:, 0:1] + log_fail[:, 0:1])
            acc_ref[...] += pv
            return c

        lax.fori_loop(0, n_kb, step, 0)
        o_ref[rows, :] = acc_ref[...].astype(BF16)


def _sb_attention(qkv, batch, seq):
    tq = min(seq, SB_Q_TILE)
    pairs = SB_W // LANES
    q0 = (SWA_Q_W + 2 * SWA_KV_W) // LANES
    k0, v0 = q0 + pairs, q0 + 2 * pairs
    return pl.pallas_call(
        functools.partial(_sb_kernel, seq),
        grid=(batch, pairs),
        in_specs=[
            pl.BlockSpec((seq, LANES), lambda b, p: (b, q0 + p)),
            pl.BlockSpec((seq, LANES), lambda b, p: (b, k0 + p)),
            pl.BlockSpec((seq, LANES), lambda b, p: (b, v0 + p)),
        ],
        out_specs=pl.BlockSpec((seq, LANES), lambda b, p: (b, p)),
        out_shape=jax.ShapeDtypeStruct((batch * seq, SB_W), BF16),
        scratch_shapes=[
            pltpu.VMEM((2, seq, LANES), BF16),
            pltpu.VMEM((2, seq, LANES), BF16),
            pltpu.VMEM((tq, LANES), F32),
            pltpu.VMEM((2, tq, LANES), F32),
        ],
        compiler_params=_params("arbitrary", "arbitrary"),
    )(qkv, qkv, qkv)


def _hgrn_kernel(seq, q_ref, f_ref, i_ref, g_ref, lb_ref, on_ref, o_ref,
                 qd_scr, kd_scr, ke_scr, kk_scr, b_scr, o_scr, st_scr, qrow_scr):
    ch = HGRN_CHUNK
    hd = HGRN_HEAD_DIM
    heads = HGRN_HEADS_PER_STEP
    width = heads * hd
    pt = min(seq, HGRN_PRE_TILE)

    r = lax.broadcasted_iota(jnp.int32, (pt, pt), 0)
    c = lax.broadcasted_iota(jnp.int32, (pt, pt), 1)
    same = (r // ch) == (c // ch)
    cum = (same & (c <= r)).astype(BF16)
    tot = same.astype(BF16)
    cum2 = jnp.concatenate([cum, cum], axis=1)
    tot2 = jnp.concatenate([tot, tot], axis=1)
    lb = lb_ref[...]

    min_b = jnp.zeros((1, width), F32)
    for t in range(seq // pt):
        rows = slice(t * pt, (t + 1) * pt)
        sig, sig_neg = _sigmoid_pair(f_ref[rows, :].astype(F32))
        log_f = jnp.log(lb + (1.0 - lb) * sig)
        kk = (1.0 - lb) * sig_neg
        hi, lo = _split_bf16(log_f)
        parts = jnp.concatenate([hi, lo], axis=0)
        b = jnp.dot(cum2, parts, preferred_element_type=F32)
        b_end = jnp.dot(tot2, parts, preferred_element_type=F32)
        q = q_ref[rows, :].astype(F32)
        qd_scr[rows, :] = (q * jnp.exp(b)).astype(BF16)
        kd_scr[rows, :] = (kk * jnp.exp(-b)).astype(BF16)
        ke_scr[rows, :] = (kk * jnp.exp(b_end - b)).astype(BF16)
        kk_scr[rows, :] = kk.astype(BF16)
        b_scr[rows, :] = b
        min_b = jnp.minimum(min_b, jnp.min(b_end, axis=0, keepdims=True))

    safe = jnp.min(min_b) > HGRN_SAFE_LOG_DECAY
    tril = (lax.broadcasted_iota(jnp.int32, (ch, ch), 1)
            <= lax.broadcasted_iota(jnp.int32, (ch, ch), 0))
    s_idx = lax.broadcasted_iota(jnp.int32, (ch, 1), 0)
    row8 = lax.broadcasted_iota(jnp.int32, (8, 1), 0)

    def chunk_body(factored, n, carry):
        r0 = pl.multiple_of(n * ch, ch)
        for h in range(heads):
            cols = slice(h * hd, (h + 1) * hd)
            qd = qd_scr[pl.ds(r0, ch), cols]
            v = i_ref[pl.ds(r0, ch), cols]
            st = st_scr[h]
            inter = lax.dot_general(qd, st.astype(BF16), NT_DIMS, preferred_element_type=F32)
            if factored:
                scores = lax.dot_general(qd, kd_scr[pl.ds(r0, ch), cols], NT_DIMS,
                                         preferred_element_type=F32)
                scores = jnp.where(tril, scores, 0.0).astype(BF16)
                o_scr[pl.ds(r0, ch), cols] = inter + jnp.dot(scores, v, preferred_element_type=F32)
            else:
                b_c = b_scr[pl.ds(r0, ch), cols]
                k_c = kk_scr[pl.ds(r0, ch), cols].astype(F32)
                v_c = v.astype(F32)
                o_scr[pl.ds(r0, ch), cols] = inter
                qrow_scr[...] = q_ref[pl.ds(r0, ch), cols].astype(F32)

                def row(t, cc, b_c=b_c, k_c=k_c, v_c=v_c, cols=cols):
                    g0 = pl.multiple_of((t // 8) * 8, 8)
                    pick = row8 == (t % 8)
                    take = lambda blk: jnp.sum(jnp.where(pick, blk, 0.0), axis=0, keepdims=True)
                    b_t = take(b_scr[pl.ds(r0 + g0, 8), cols])
                    q_t = take(qrow_scr[pl.ds(g0, 8), :])
                    w = jnp.where(s_idx <= t, jnp.exp(jnp.minimum(b_t - b_c, 0.0)), 0.0)
                    sc = jnp.sum(q_t * k_c * w, axis=-1, keepdims=True)
                    o_t = jnp.sum(sc * v_c, axis=0, keepdims=True)
                    o_scr[pl.ds(r0 + g0, 8), cols] += jnp.where(pick, o_t, 0.0)
                    return cc

                lax.fori_loop(0, ch, row, 0)
            decay = jnp.exp(b_scr[pl.ds(r0 + ch - 8, 8), cols][7:8, :])
            kv = lax.dot_general(v, ke_scr[pl.ds(r0, ch), cols], TN_DIMS,
                                 preferred_element_type=F32)
            st_scr[h] = st * decay + kv
        return carry

    st_scr[...] = jnp.zeros_like(st_scr)

    @pl.when(safe)
    def _():
        lax.fori_loop(0, seq // ch, functools.partial(chunk_body, True), 0)

    @pl.when(jnp.logical_not(safe))
    def _():
        lax.fori_loop(0, seq // ch, functools.partial(chunk_body, False), 0)

    for t in range(seq // pt):
        rows = slice(t * pt, (t + 1) * pt)
        gate = g_ref[rows, :].astype(F32)
        sig, _ = _sigmoid_pair(gate)
        for h in range(heads):
            cols = slice(h * hd, (h + 1) * hd)
            y = _rms(o_scr[rows, cols], on_ref[...])
            o_ref[rows, cols] = (y * (gate[:, cols] * sig[:, cols])).astype(BF16)


def _hgrn_mixer_core(proj, lower_bound, o_gain, batch, seq):
    heads = HGRN_HEADS_PER_STEP
    width = heads * HGRN_HEAD_DIM
    groups = HGRN_HEADS // heads
    section = lambda k: pl.BlockSpec((seq, width), lambda b, g, k=k: (b, k * groups + g))
    return pl.pallas_call(
        functools.partial(_hgrn_kernel, seq),
        grid=(batch, groups),
        in_specs=[section(0), section(1), section(2), section(3),
                  pl.BlockSpec((1, width), lambda b, g: (0, g)),
                  pl.BlockSpec((1, HGRN_HEAD_DIM), lambda b, g: (0, 0))],
        out_specs=pl.BlockSpec((seq, width), lambda b, g: (b, g)),
        out_shape=jax.ShapeDtypeStruct((batch * seq, D_MODEL), BF16),
        scratch_shapes=[
            pltpu.VMEM((seq, width), BF16),
            pltpu.VMEM((seq, width), BF16),
            pltpu.VMEM((seq, width), BF16),
            pltpu.VMEM((seq, width), BF16),
            pltpu.VMEM((seq, width), F32),
            pltpu.VMEM((seq, width), F32),
            pltpu.VMEM((heads, HGRN_HEAD_DIM, HGRN_HEAD_DIM), F32),
            pltpu.VMEM((HGRN_CHUNK, HGRN_HEAD_DIM), F32),
        ],
        compiler_params=_params("arbitrary", "arbitrary"),
    )(proj, proj, proj, proj, lower_bound.reshape(1, D_MODEL), o_gain.reshape(1, HGRN_HEAD_DIM))


def _xa_kv_kernel(mem_ref, g_ref, w_ref, kn_ref, k_ref, v_ref):
    mem_n = _rms(mem_ref[...], g_ref[...]).astype(BF16)
    kv = jnp.dot(mem_n, w_ref[...], preferred_element_type=F32)
    for h in range(XA_HEADS):
        cols = slice(h * XA_HEAD_DIM, (h + 1) * XA_HEAD_DIM)
        k_ref[:, cols] = _rms(kv[:, cols], kn_ref[...]).astype(BF16)
    v_ref[...] = kv[:, D_MODEL:].astype(BF16)


def _xa_kv(mem, gain, w_kv, k_gain):
    m, d = mem.shape
    tm = min(m, TOKEN_TILE)
    row = lambda i: (i, 0)
    return pl.pallas_call(
        _xa_kv_kernel,
        grid=(m // tm,),
        in_specs=[pl.BlockSpec((tm, d), row), pl.BlockSpec((1, d), lambda i: (0, 0)),
                  _resident(w_kv.shape), pl.BlockSpec((1, XA_HEAD_DIM), lambda i: (0, 0))],
        out_specs=[pl.BlockSpec((tm, d), row), pl.BlockSpec((tm, d), row)],
        out_shape=[jax.ShapeDtypeStruct((m, d), BF16)] * 2,
        compiler_params=_params("arbitrary"),
    )(mem, gain.reshape(1, d), w_kv, k_gain.reshape(1, XA_HEAD_DIM))


def _xa_kernel(x_ref, g_ref, wq_ref, qn_ref, k_ref, v_ref, wo_ref, o_ref):
    x = x_ref[...]
    h = _rms(x, g_ref[...]).astype(BF16)
    q = jnp.dot(h, wq_ref[...], preferred_element_type=F32)
    heads = []
    for hd in range(XA_HEADS):
        cols = slice(hd * XA_HEAD_DIM, (hd + 1) * XA_HEAD_DIM)
        qh = (_rms(q[:, cols], qn_ref[...]) * XA_HEAD_DIM ** -0.5).astype(BF16)
        s = lax.dot_general(qh, k_ref[:, cols], NT_DIMS, preferred_element_type=F32)
        p = jnp.exp(s - jnp.max(s, axis=-1, keepdims=True))
        p = p * (1.0 / jnp.sum(p, axis=-1, keepdims=True))
        heads.append(jnp.dot(p.astype(BF16), v_ref[:, cols],
                             preferred_element_type=F32).astype(BF16))
    o = jnp.concatenate(heads, axis=1)
    o_ref[...] = x + jnp.dot(o, wo_ref[...], preferred_element_type=F32)


def _cross_attention(x, gain, w_q, q_gain, k, v, w_o, seq, n_mem):
    m, d = x.shape
    tm = min(seq, TOKEN_TILE)
    per_seq = seq // tm
    row = lambda i: (i, 0)
    mem_row = lambda i: (i // per_seq, 0)
    return pl.pallas_call(
        _xa_kernel,
        grid=(m // tm,),
        in_specs=[pl.BlockSpec((tm, d), row), pl.BlockSpec((1, d), lambda i: (0, 0)),
                  _resident(w_q.shape), pl.BlockSpec((1, XA_HEAD_DIM), lambda i: (0, 0)),
                  pl.BlockSpec((n_mem, d), mem_row), pl.BlockSpec((n_mem, d), mem_row),
                  _resident(w_o.shape)],
        out_specs=pl.BlockSpec((tm, d), row),
        out_shape=jax.ShapeDtypeStruct((m, d), F32),
        compiler_params=_params("arbitrary"),
    )(x, gain.reshape(1, d), w_q, q_gain.reshape(1, XA_HEAD_DIM), k, v, w_o)


def _ffn_kernel(per_seq, d_ff, x_ref, g_ref, wup_ref, cw_ref, cb_ref, wdn_ref, o_ref, halo_ref):
    tm = x_ref.shape[0]
    fc = FF_CHUNK
    n_chunks = d_ff // fc

    @pl.when(pl.program_id(0) % per_seq == 0)
    def _():
        halo_ref[...] = jnp.zeros_like(halo_ref)

    x = x_ref[...]
    h = _rms(x, g_ref[...]).astype(BF16)
    row8 = lax.broadcasted_iota(jnp.int32, (8, fc), 0)

    def conv_branch(col0, slot):
        cols = slice(col0, col0 + fc)
        u = jnp.dot(h, wup_ref[:, cols], preferred_element_type=F32)
        prev = halo_ref[slot]
        halo_ref[slot] = u[tm - 8:, :]
        out = cw_ref[2:3, cols] * u + cb_ref[:, cols]
        for shift in (1, 2):
            rolled = pltpu.roll(u, shift, 0)
            head = jnp.where(row8 < shift, pltpu.roll(prev, shift, 0), rolled[:8])
            shifted = jnp.concatenate([head, rolled[8:]], axis=0)
            out = out + cw_ref[2 - shift:3 - shift, cols] * shifted
        return out

    acc = x
    for j in range(n_chunks):
        gate = conv_branch(j * fc, j)
        up = conv_branch(d_ff + j * fc, n_chunks + j)
        sig, _ = _sigmoid_pair(gate)
        act = (gate * sig * up).astype(BF16)
        acc = acc + jnp.dot(act, wdn_ref[j * fc:(j + 1) * fc, :], preferred_element_type=F32)
    o_ref[...] = acc


def _conv_ffn(x, gain, w_up, conv_w, conv_b, w_down, seq):
    m, d = x.shape
    d_ff = w_down.shape[0]
    assert d_ff % FF_CHUNK == 0
    tm = min(seq, TOKEN_TILE)
    row = lambda i: (i, 0)
    const = lambda i: (0, 0)
    return pl.pallas_call(
        functools.partial(_ffn_kernel, seq // tm, d_ff),
        grid=(m // tm,),
        in_specs=[pl.BlockSpec((tm, d), row), pl.BlockSpec((1, d), const),
                  _resident(w_up.shape), pl.BlockSpec(conv_w.shape, const),
                  pl.BlockSpec((1, 2 * d_ff), const), _resident(w_down.shape)],
        out_specs=pl.BlockSpec((tm, d), row),
        out_shape=jax.ShapeDtypeStruct((m, d), F32),
        scratch_shapes=[pltpu.VMEM((2 * d_ff // FF_CHUNK, 8, FF_CHUNK), F32)],
        compiler_params=_params("arbitrary"),
    )(x, gain.reshape(1, d), w_up, conv_w, conv_b.reshape(1, 2 * d_ff), w_down)


def kernel(x, mem, positions, norm_mix, norm_cross, norm_mem, norm_ffn, ab_w_in, ab_w_out, swa_q_norm, swa_k_norm, swa_sinks, hgrn_w_in, hgrn_w_out, hgrn_o_norm, hgrn_lb, xa_w_q, xa_w_kv, xa_w_o, xa_q_norm, xa_k_norm, ffn_w_up, ffn_conv_w, ffn_conv_b, ffn_w_down):
    batch, seq, d = x.shape
    n_mem = mem.shape[1]
    depth = norm_mix.shape[0]
    assert d == D_MODEL and seq % TOKEN_TILE == 0
    assert (batch * n_mem) % min(batch * n_mem, TOKEN_TILE) == 0
    bf = lambda w: w.astype(BF16)

    p_lb = jax.nn.softmax(hgrn_lb.astype(F32), axis=0)
    lower_bounds = jnp.cumsum(p_lb, axis=0) - p_lb[0]

    xf = x.reshape(batch * seq, d)
    memf = mem.reshape(batch * n_mem, d)
    cos, sin = _rope_tables(positions)

    for l in range(depth):
        if l % 2 == 0:
            e = l // 2
            qkv = _norm_matmul(xf, norm_mix[l], bf(ab_w_in[e]), tn=AB_IN // 3)
            out_a = _swa_attention(qkv, cos, sin, swa_q_norm[e], swa_k_norm[e], swa_sinks[e],
                                   batch, seq)
            out_b = _sb_attention(qkv, batch, seq)
            w_out = bf(ab_w_out[e])
            xf = _proj_residual(xf, [out_a, out_b], [w_out[:SWA_Q_W], w_out[SWA_Q_W:]])
        else:
            o = l // 2
            proj = _norm_matmul(xf, norm_mix[l], bf(hgrn_w_in[o]), tn=D_MODEL)
            mixed = _hgrn_mixer_core(proj, lower_bounds[l], hgrn_o_norm[o], batch, seq)
            xf = _proj_residual(xf, [mixed], [bf(hgrn_w_out[o])])
        k, v = _xa_kv(memf, norm_mem[l], bf(xa_w_kv[l]), xa_k_norm[l])
        xf = _cross_attention(xf, norm_cross[l], bf(xa_w_q[l]), xa_q_norm[l], k, v,
                              bf(xa_w_o[l]), seq, n_mem)
        xf = _conv_ffn(xf, norm_ffn[l], bf(ffn_w_up[l]), ffn_conv_w[l], ffn_conv_b[l],
                       bf(ffn_w_down[l]), seq)
    return xf.reshape(batch, seq, d)
```

```python
import functools

import jax
import jax.numpy as jnp
from jax import lax
from jax.experimental import pallas as pl
from jax.experimental.pallas import tpu as pltpu

F32 = jnp.float32
BF16 = jnp.bfloat16

D_MODEL = 1024
HEAD_DIM = 64
SWA_HEADS = 8
SWA_KV_HEADS = 2
WINDOW = 128
SB_HEADS = 8
SB_BLOCK = 128
HGRN_HEAD_DIM = 128
HGRN_HEADS = D_MODEL // HGRN_HEAD_DIM
XA_HEADS = 4
XA_HEAD_DIM = D_MODEL // XA_HEADS
ROPE_THETA = 10000.0
EPS = 1e-6
SWA_Q_W = SWA_HEADS * HEAD_DIM
SWA_KV_W = SWA_KV_HEADS * HEAD_DIM
SB_W = SB_HEADS * HEAD_DIM
AB_IN = SWA_Q_W + 2 * SWA_KV_W + 3 * SB_W

LANES = 128
MXU_DIM = 256
TOKEN_TILE = 512
FF_CHUNK = 256
HGRN_CHUNK = 32
HGRN_STATE_CHUNK = 64
HGRN_PRE_TILE = 256
HGRN_HEADS_PER_STEP = 4
HGRN_SAFE_LOG_DECAY = -60.0
SB_Q_TILE = 512
VMEM_LIMIT = 56 * 1024 * 1024

NT_DIMS = (((1,), (1,)), ((), ()))
TN_DIMS = (((0,), (0,)), ((), ()))


def _params(*sem):
    return pltpu.CompilerParams(dimension_semantics=sem, vmem_limit_bytes=VMEM_LIMIT)


def _resident(shape):
    zeros = (0,) * len(shape)
    return pl.BlockSpec(shape, lambda *_: zeros, pipeline_mode=pl.Buffered(1))


def _rms(x, gain):
    ms = jnp.mean(x * x, axis=-1, keepdims=True)
    return x * lax.rsqrt(ms + EPS) * gain


def _split_bf16(x):
    hi = x.astype(BF16)
    lo = (x - hi.astype(F32)).astype(BF16)
    return hi, lo


def _sigmoid_pair(x):
    e = jnp.exp(-jnp.abs(x))
    big = 1.0 / (1.0 + e)
    small = e * big
    pos = x >= 0
    return jnp.where(pos, big, small), jnp.where(pos, small, big)


def _norm_matmul_kernel(tn, x_ref, g_ref, w_ref, o_ref):
    h = _rms(x_ref[...], g_ref[...]).astype(BF16)
    for j in range(w_ref.shape[1] // tn):
        cols = slice(j * tn, (j + 1) * tn)
        o_ref[:, cols] = jnp.dot(h, w_ref[:, cols], preferred_element_type=F32).astype(BF16)


def _norm_matmul(x, gain, w, tn):
    m, d = x.shape
    n = w.shape[1]
    assert n % tn == 0
    tm = TOKEN_TILE
    row = lambda i: (i, 0)
    return pl.pallas_call(
        functools.partial(_norm_matmul_kernel, tn),
        grid=(m // tm,),
        in_specs=[pl.BlockSpec((tm, d), row), pl.BlockSpec((1, d), lambda i: (0, 0)),
                  _resident(w.shape)],
        out_specs=pl.BlockSpec((tm, n), row),
        out_shape=jax.ShapeDtypeStruct((m, n), BF16),
        compiler_params=_params("arbitrary"),
    )(x, gain.reshape(1, d), w)


def _rope_table_kernel(pos_ref, inv_ref, cos_ref, sin_ref):
    ang = pos_ref[...] * inv_ref[...]
    lane = lax.broadcasted_iota(jnp.int32, ang.shape, 1)
    first_half = (lane % HEAD_DIM) < HEAD_DIM // 2
    cos_ref[...] = jnp.cos(ang)
    sin = jnp.sin(ang)
    sin_ref[...] = jnp.where(first_half, -sin, sin)


def _rope_tables(positions):
    m = positions.size
    tm = TOKEN_TILE
    half = HEAD_DIM // 2
    inv_freq = ROPE_THETA ** (-jnp.arange(0, HEAD_DIM, 2, dtype=F32) / HEAD_DIM)
    inv = jnp.tile(inv_freq, LANES // half).reshape(1, LANES)
    pos = positions.astype(F32).reshape(m, 1)
    row = lambda i: (i, 0)
    return pl.pallas_call(
        _rope_table_kernel,
        grid=(m // tm,),
        in_specs=[pl.BlockSpec((tm, 1), row), pl.BlockSpec((1, LANES), lambda i: (0, 0))],
        out_specs=[pl.BlockSpec((tm, LANES), row), pl.BlockSpec((tm, LANES), row)],
        out_shape=[jax.ShapeDtypeStruct((m, LANES), F32)] * 2,
        compiler_params=_params("arbitrary"),
    )(pos, inv)


def _group_ones(width):
    r = lax.broadcasted_iota(jnp.int32, (width, width), 0) // HEAD_DIM
    c = lax.broadcasted_iota(jnp.int32, (width, width), 1) // HEAD_DIM
    return (r == c).astype(BF16)


def _head_norm_rope(x, gain, cos, sin):
    hi, lo = _split_bf16(x * x)
    ones = _group_ones(LANES)
    ss = (jnp.dot(hi, ones, preferred_element_type=F32)
          + jnp.dot(lo, ones, preferred_element_type=F32))
    y = x * lax.rsqrt(ss * (1.0 / HEAD_DIM) + EPS) * gain
    lane = lax.broadcasted_iota(jnp.int32, y.shape, 1)
    first_half = (lane % HEAD_DIM) < HEAD_DIM // 2
    partner = jnp.where(first_half,
                        pltpu.roll(y, LANES - HEAD_DIM // 2, 1),
                        pltpu.roll(y, HEAD_DIM // 2, 1))
    return y * cos + partner * sin


def _swa_kernel(seq, qkv_ref, cos_ref, sin_ref, qg_ref, kg_ref, sink_ref, o_ref,
                q_scr, k_scr, v_scr):
    rt = min(seq, TOKEN_TILE)
    n_q_chunks = SWA_Q_W // LANES
    for r in range(seq // rt):
        rows = slice(r * rt, (r + 1) * rt)
        cos, sin = cos_ref[rows, :], sin_ref[rows, :]
        for c in range(n_q_chunks):
            cols = slice(c * LANES, (c + 1) * LANES)
            xq = qkv_ref[rows, cols].astype(F32)
            q = _head_norm_rope(xq, qg_ref[...], cos, sin)
            q_scr[c, rows, :] = (q * HEAD_DIM ** -0.5).astype(BF16)
        xk = qkv_ref[rows, SWA_Q_W:SWA_Q_W + SWA_KV_W].astype(F32)
        k = _head_norm_rope(xk, kg_ref[...], cos, sin)
        v = qkv_ref[rows, SWA_Q_W + SWA_KV_W:SWA_Q_W + 2 * SWA_KV_W].astype(F32)
        low = lax.broadcasted_iota(jnp.int32, k.shape, 1) < HEAD_DIM
        for t, scr in ((k, k_scr), (v, v_scr)):
            swapped = pltpu.roll(t, HEAD_DIM, 1)
            scr[0, 0, rows, :] = jnp.where(low, t, 0.0).astype(BF16)
            scr[1, 0, rows, :] = jnp.where(low, 0.0, swapped).astype(BF16)
            scr[0, 1, rows, :] = jnp.where(low, swapped, 0.0).astype(BF16)
            scr[1, 1, rows, :] = jnp.where(low, 0.0, t).astype(BF16)

    w = WINDOW
    qi = lax.broadcasted_iota(jnp.int32, (2 * w, w), 0) % w
    kj = lax.broadcasted_iota(jnp.int32, (2 * w, w), 1)
    top = lax.broadcasted_iota(jnp.int32, (2 * w, 1), 0) < w
    cur_mask = kj <= qi

    def block(n, carry):
        r0 = pl.multiple_of(n * w, w)
        p0 = pl.multiple_of(jnp.maximum(n - 1, 0) * w, w)
        prev_mask = (kj > qi) & (n > 0)
        for g in range(SWA_KV_HEADS):
            qa = jnp.concatenate([q_scr[2 * g, pl.ds(r0, w), :],
                                  q_scr[2 * g + 1, pl.ds(r0, w), :]], axis=0)
            out = jnp.zeros((2 * w, LANES), F32)
            for slot in range(2):
                s_cur = lax.dot_general(qa, k_scr[slot, g, pl.ds(r0, w), :], NT_DIMS,
                                        preferred_element_type=F32)
                s_prev = lax.dot_general(qa, k_scr[slot, g, pl.ds(p0, w), :], NT_DIMS,
                                         preferred_element_type=F32)
                s_cur = jnp.where(cur_mask, s_cur, -jnp.inf)
                s_prev = jnp.where(prev_mask, s_prev, -jnp.inf)
                sink = jnp.where(top, sink_ref[4 * g + slot], sink_ref[4 * g + 2 + slot])
                m = jnp.maximum(jnp.maximum(jnp.max(s_cur, axis=-1, keepdims=True),
                                            jnp.max(s_prev, axis=-1, keepdims=True)), sink)
                p_cur = jnp.exp(s_cur - m)
                p_prev = jnp.exp(s_prev - m)
                den = (jnp.sum(p_cur, axis=-1, keepdims=True)
                       + jnp.sum(p_prev, axis=-1, keepdims=True) + jnp.exp(sink - m))
                inv = 1.0 / den
                out = out + jnp.dot((p_cur * inv).astype(BF16), v_scr[slot, g, pl.ds(r0, w), :],
                                    preferred_element_type=F32)
                out = out + jnp.dot((p_prev * inv).astype(BF16), v_scr[slot, g, pl.ds(p0, w), :],
                                    preferred_element_type=F32)
            o_ref[pl.ds(r0, w), 2 * g * LANES:(2 * g + 1) * LANES] = out[:w].astype(BF16)
            o_ref[pl.ds(r0, w), (2 * g + 1) * LANES:(2 * g + 2) * LANES] = out[w:].astype(BF16)
        return carry

    lax.fori_loop(0, seq // w, block, 0)


def _swa_attention(qkv, cos, sin, q_gain, k_gain, sinks, batch, seq):
    width = SWA_Q_W + 2 * SWA_KV_W
    assert AB_IN % width == 0
    const = lambda b: (0, 0)
    return pl.pallas_call(
        functools.partial(_swa_kernel, seq),
        grid=(batch,),
        in_specs=[
            pl.BlockSpec((seq, width), lambda b: (b, 0)),
            pl.BlockSpec((seq, LANES), lambda b: (b, 0)),
            pl.BlockSpec((seq, LANES), lambda b: (b, 0)),
            pl.BlockSpec((1, LANES), const),
            pl.BlockSpec((1, LANES), const),
            pl.BlockSpec(memory_space=pltpu.SMEM),
        ],
        out_specs=pl.BlockSpec((seq, SWA_Q_W), lambda b: (b, 0)),
        out_shape=jax.ShapeDtypeStruct((batch * seq, SWA_Q_W), BF16),
        scratch_shapes=[
            pltpu.VMEM((SWA_Q_W // LANES, seq, LANES), BF16),
            pltpu.VMEM((2, SWA_KV_HEADS, seq, LANES), BF16),
            pltpu.VMEM((2, SWA_KV_HEADS, seq, LANES), BF16),
        ],
        compiler_params=_params("arbitrary"),
    )(qkv, cos, sin,
      jnp.tile(q_gain, LANES // HEAD_DIM).reshape(1, LANES),
      jnp.tile(k_gain, LANES // HEAD_DIM).reshape(1, LANES),
      sinks)


def _neg_abs(x):
    bits = lax.bitcast_convert_type(x, jnp.uint32) | jnp.uint32(0x80000000)
    return lax.bitcast_convert_type(bits, F32)


def _sb_kernel(seq, q_ref, k_ref, v_ref, o_ref, k_scr, v_scr, later_scr, acc_ref, carry_ref):
    blk = SB_BLOCK
    tq = min(seq, SB_Q_TILE)
    per_tile = tq // blk
    nb = seq // blk
    low = lax.broadcasted_iota(jnp.int32, (seq, LANES), 1) < HEAD_DIM
    for src, scr in ((k_ref, k_scr), (v_ref, v_scr)):
        full = src[...]
        zero = jnp.zeros_like(full)
        scr[:, 0:blk, :] = jnp.where(low, full, zero).reshape(nb, blk, LANES)
        scr[:, blk:2 * blk, :] = jnp.where(low, zero, full).reshape(nb, blk, LANES)

    r = lax.broadcasted_iota(jnp.int32, (4 * blk, 2 * blk), 0)
    c = lax.broadcasted_iota(jnp.int32, (4 * blk, 2 * blk), 1)
    later_scr[...] = (((r // blk) % 2 == c // blk) & (r % blk > c % blk)).astype(BF16)

    def step(q_rows, n_rows, kb, masked):
        qt = q_ref[pl.ds(q_rows, n_rows), :] * jnp.asarray(HEAD_DIM ** -0.5, BF16)
        z = lax.dot_general(qt, k_scr[kb], NT_DIMS, preferred_element_type=F32)
        log1p_e = jnp.log(1.0 + jnp.exp(_neg_abs(z)))
        log_beta = jnp.minimum(z, 0.0) - log1p_e
        log_fail = log_beta - z
        if masked:
            key = lax.broadcasted_iota(jnp.int32, (n_rows, 2 * blk), 1) % blk
            qry = lax.broadcasted_iota(jnp.int32, (n_rows, 2 * blk), 0)
            valid = key < qry
            log_fail = jnp.where(valid, log_fail, 0.0)
        hi, lo = _split_bf16(log_fail)
        after = jnp.dot(jnp.concatenate([hi, lo], axis=1), later_scr[...],
                        preferred_element_type=F32)
        return z, log_beta, log_fail, after, (valid if masked else None)

    def finish(t_rows, n_rows, kb, parts):
        z, log_beta, log_fail, after, valid = parts
        rows = pl.ds(t_rows, n_rows)
        carry = carry_ref[rows, :]
        a = jnp.exp(log_beta + after + carry)
        if valid is not None:
            a = jnp.where(valid, a, 0.0)
        acc_ref[rows, :] += jnp.dot(a.astype(BF16), v_scr[kb], preferred_element_type=F32)
        for h in range(2):
            cols = slice(h * blk, (h + 1) * blk)
            total = after[:, h * blk:h * blk + 1] + log_fail[:, h * blk:h * blk + 1]
            carry_ref[rows, cols] = carry[:, cols] + total

    def tile(qi, carry_):
        row0 = pl.multiple_of(qi * tq, tq)
        first = qi * per_tile
        acc_ref[...] = jnp.zeros_like(acc_ref)
        carry_ref[...] = jnp.zeros_like(carry_ref)
        for j in reversed(range(per_tile)):
            n_rows = tq - j * blk
            finish(j * blk, n_rows, first + j, step(row0 + j * blk, n_rows, first + j, True))

        def below(i, c_):
            kbs = [first - 1 - (per_tile * i + u) for u in range(per_tile)]
            parts = [step(row0, tq, kb, False) for kb in kbs]
            for kb, p in zip(kbs, parts):
                finish(0, tq, kb, p)
            return c_

        lax.fori_loop(0, qi, below, 0)
        o_ref[pl.ds(row0, tq), :] = acc_ref[...].astype(BF16)
        return carry_

    lax.fori_loop(0, seq // tq, tile, 0)


def _sb_attention(qkv, batch, seq):
    tq = min(seq, SB_Q_TILE)
    pairs = SB_W // LANES
    q0 = (SWA_Q_W + 2 * SWA_KV_W) // LANES
    k0, v0 = q0 + pairs, q0 + 2 * pairs
    return pl.pallas_call(
        functools.partial(_sb_kernel, seq),
        grid=(batch, pairs),
        in_specs=[
            pl.BlockSpec((seq, LANES), lambda b, p: (b, q0 + p)),
            pl.BlockSpec((seq, LANES), lambda b, p: (b, k0 + p)),
            pl.BlockSpec((seq, LANES), lambda b, p: (b, v0 + p)),
        ],
        out_specs=pl.BlockSpec((seq, LANES), lambda b, p: (b, p)),
        out_shape=jax.ShapeDtypeStruct((batch * seq, SB_W), BF16),
        scratch_shapes=[
            pltpu.VMEM((seq // SB_BLOCK, 2 * SB_BLOCK, LANES), BF16),
            pltpu.VMEM((seq // SB_BLOCK, 2 * SB_BLOCK, LANES), BF16),
            pltpu.VMEM((4 * SB_BLOCK, 2 * SB_BLOCK), BF16),
            pltpu.VMEM((tq, LANES), F32),
            pltpu.VMEM((tq, 2 * SB_BLOCK), F32),
        ],
        compiler_params=_params("arbitrary", "arbitrary"),
    )(qkv, qkv, qkv)


def _hgrn_kernel(seq, q_ref, f_ref, i_ref, g_ref, lb_ref, on_ref, o_ref,
                 qs_scr, ks_scr, b_scr, o_scr, dec_scr, st_scr, qrow_scr):
    ch = HGRN_CHUNK
    sc = HGRN_STATE_CHUNK
    hd = HGRN_HEAD_DIM
    heads = HGRN_HEADS_PER_STEP
    width = heads * hd
    pt = min(seq, HGRN_PRE_TILE)

    r = lax.broadcasted_iota(jnp.int32, (pt, pt), 0)
    c = lax.broadcasted_iota(jnp.int32, (pt, pt), 1)
    same_sub = (r // ch) == (c // ch)
    same_chunk = (r // sc) == (c // sc)
    m_diag = same_sub & (c <= r)
    m_prev = same_chunk & ((r // ch) == (c // ch) + 1)
    twice = lambda m: jnp.concatenate([m.astype(BF16)] * 2, axis=1)
    cum_sub, tot_sub, tot_chunk = twice(m_diag), twice(same_sub), twice(same_chunk)
    first_sub = (lax.broadcasted_iota(jnp.int32, (pt, 1), 0) // ch) % (sc // ch) == 0
    s_idx = lax.broadcasted_iota(jnp.int32, (ch, 1), 0)
    row8 = lax.broadcasted_iota(jnp.int32, (8, 1), 0)
    lb = lb_ref[...]

    def exact_diagonal(row0):
        def sub_chunk(n, carry):
            r0 = pl.multiple_of(row0 + n * ch, ch)
            for h in range(heads):
                cols = slice(h * hd, (h + 1) * hd)
                b_c = b_scr[pl.ds(r0, ch), cols]
                _, sig_neg = _sigmoid_pair(f_ref[pl.ds(r0, ch), cols].astype(F32))
                k_c = (1.0 - lb[:, cols]) * sig_neg
                v_c = i_ref[pl.ds(r0, ch), cols].astype(F32)
                qrow_scr[...] = q_ref[pl.ds(r0, ch), cols].astype(F32)

                def row(t, cc, b_c=b_c, k_c=k_c, v_c=v_c, cols=cols):
                    g0 = pl.multiple_of((t // 8) * 8, 8)
                    pick = row8 == (t % 8)
                    take = lambda blk: jnp.sum(jnp.where(pick, blk, 0.0), axis=0, keepdims=True)
                    b_t = take(b_scr[pl.ds(r0 + g0, 8), cols])
                    q_t = take(qrow_scr[pl.ds(g0, 8), :])
                    w = jnp.where(s_idx <= t, jnp.exp(jnp.minimum(b_t - b_c, 0.0)), 0.0)
                    sc_t = jnp.sum(q_t * k_c * w, axis=-1, keepdims=True)
                    o_t = jnp.sum(sc_t * v_c, axis=0, keepdims=True)
                    o_scr[pl.ds(r0 + g0, 8), cols] += jnp.where(pick, o_t, 0.0)
                    return cc

                lax.fori_loop(0, ch, row, 0)
            return carry

        lax.fori_loop(0, pt // ch, sub_chunk, 0)

    for t in range(seq // pt):
        rows = slice(t * pt, (t + 1) * pt)
        sig, sig_neg = _sigmoid_pair(f_ref[rows, :].astype(F32))
        log_f = jnp.log(lb + (1.0 - lb) * sig)
        kk = (1.0 - lb) * sig_neg
        hi, lo = _split_bf16(log_f)
        parts = jnp.concatenate([hi, lo], axis=0)
        b = jnp.dot(cum_sub, parts, preferred_element_type=F32)
        end_sub = jnp.dot(tot_sub, parts, preferred_element_type=F32)
        end_chunk = jnp.dot(tot_chunk, parts, preferred_element_type=F32)
        before = jnp.where(first_sub, 0.0, end_chunk - end_sub)
        q = q_ref[rows, :].astype(F32)
        qd = q * jnp.exp(b)
        qd_b = qd.astype(BF16)
        kd_b = (kk * jnp.exp(-b)).astype(BF16)
        ke_b = (kk * jnp.exp(end_sub - b)).astype(BF16)
        qs_scr[rows, :] = (qd * jnp.exp(before)).astype(BF16)
        ks_scr[rows, :] = (kk * jnp.exp(end_chunk - b - before)).astype(BF16)
        b_scr[rows, :] = b
        for i in range(pt // sc):
            dec_scr[t * (pt // sc) + i] = jnp.broadcast_to(
                jnp.exp(end_chunk[i * sc:i * sc + 1, :]), (8, width))
        safe = jnp.min(end_sub) > HGRN_SAFE_LOG_DECAY
        v = i_ref[rows, :]
        for h in range(heads):
            cols = slice(h * hd, (h + 1) * hd)
            a_diag = lax.dot_general(qd_b[:, cols], kd_b[:, cols], NT_DIMS,
                                     preferred_element_type=F32)
            a_prev = lax.dot_general(qd_b[:, cols], ke_b[:, cols], NT_DIMS,
                                     preferred_element_type=F32)
            a = jnp.where(m_diag, jnp.where(safe, a_diag, 0.0), jnp.where(m_prev, a_prev, 0.0))
            o_scr[rows, cols] = jnp.dot(a.astype(BF16), v[:, cols], preferred_element_type=F32)

        @pl.when(jnp.logical_not(safe))
        def _(t=t):
            exact_diagonal(t * pt)

    st_scr[...] = jnp.zeros_like(st_scr)

    def state_step(n, carry):
        r0 = pl.multiple_of(n * sc, sc)
        decay = dec_scr[n]
        for h in range(heads):
            cols = slice(h * hd, (h + 1) * hd)
            st = st_scr[h]
            o_scr[pl.ds(r0, sc), cols] += lax.dot_general(
                qs_scr[pl.ds(r0, sc), cols], st.astype(BF16), NT_DIMS, preferred_element_type=F32)
            kv = lax.dot_general(i_ref[pl.ds(r0, sc), cols], ks_scr[pl.ds(r0, sc), cols], TN_DIMS,
                                 preferred_element_type=F32)
            st_scr[h] = st * decay[0:1, cols] + kv
        return carry

    lax.fori_loop(0, seq // sc, state_step, 0, unroll=2)

    for t in range(seq // pt):
        rows = slice(t * pt, (t + 1) * pt)
        gate = g_ref[rows, :].astype(F32)
        silu = gate * (1.0 / (1.0 + jnp.exp(-gate)))
        for h in range(heads):
            cols = slice(h * hd, (h + 1) * hd)
            y = _rms(o_scr[rows, cols], on_ref[...])
            o_ref[rows, cols] = (y * silu[:, cols]).astype(BF16)


def _hgrn_mixer_core(proj, lower_bound, o_gain, batch, seq):
    heads = HGRN_HEADS_PER_STEP
    width = heads * HGRN_HEAD_DIM
    groups = HGRN_HEADS // heads
    assert seq % HGRN_PRE_TILE == 0 and HGRN_PRE_TILE % HGRN_STATE_CHUNK == 0
    section = lambda k: pl.BlockSpec((seq, width), lambda b, g, k=k: (b, k * groups + g))
    return pl.pallas_call(
        functools.partial(_hgrn_kernel, seq),
        grid=(batch, groups),
        in_specs=[section(0), section(1), section(2), section(3),
                  pl.BlockSpec((1, width), lambda b, g: (0, g)),
                  pl.BlockSpec((1, HGRN_HEAD_DIM), lambda b, g: (0, 0))],
        out_specs=pl.BlockSpec((seq, width), lambda b, g: (b, g)),
        out_shape=jax.ShapeDtypeStruct((batch * seq, D_MODEL), BF16),
        scratch_shapes=[
            pltpu.VMEM((seq, width), BF16),
            pltpu.VMEM((seq, width), BF16),
            pltpu.VMEM((seq, width), F32),
            pltpu.VMEM((seq, width), F32),
            pltpu.VMEM((seq // HGRN_STATE_CHUNK, 8, width), F32),
            pltpu.VMEM((heads, HGRN_HEAD_DIM, HGRN_HEAD_DIM), F32),
            pltpu.VMEM((HGRN_CHUNK, HGRN_HEAD_DIM), F32),
        ],
        compiler_params=_params("arbitrary", "arbitrary"),
    )(proj, proj, proj, proj, lower_bound.reshape(1, D_MODEL), o_gain.reshape(1, HGRN_HEAD_DIM))


def _xa_kv_kernel(mem_ref, g_ref, w_ref, kn_ref, k_ref, v_ref):
    mem_n = _rms(mem_ref[...], g_ref[...]).astype(BF16)
    kv = jnp.dot(mem_n, w_ref[...], preferred_element_type=F32)
    for h in range(XA_HEADS):
        cols = slice(h * XA_HEAD_DIM, (h + 1) * XA_HEAD_DIM)
        k_ref[:, cols] = _rms(kv[:, cols], kn_ref[...]).astype(BF16)
    v_ref[...] = kv[:, D_MODEL:].astype(BF16)


def _xa_kv(mem, gain, w_kv, k_gain):
    m, d = mem.shape
    tm = min(m, TOKEN_TILE)
    row = lambda i: (i, 0)
    return pl.pallas_call(
        _xa_kv_kernel,
        grid=(m // tm,),
        in_specs=[pl.BlockSpec((tm, d), row), pl.BlockSpec((1, d), lambda i: (0, 0)),
                  _resident(w_kv.shape), pl.BlockSpec((1, XA_HEAD_DIM), lambda i: (0, 0))],
        out_specs=[pl.BlockSpec((tm, d), row), pl.BlockSpec((tm, d), row)],
        out_shape=[jax.ShapeDtypeStruct((m, d), BF16)] * 2,
        compiler_params=_params("arbitrary"),
    )(mem, gain.reshape(1, d), w_kv, k_gain.reshape(1, XA_HEAD_DIM))


def _xa_kernel(n_in, x_ref, *refs):
    a_refs, w_refs = refs[:n_in], refs[n_in:2 * n_in]
    g_ref, wq_ref, qn_ref, k_ref, v_ref, wo_ref, o_ref = refs[2 * n_in:]
    x = x_ref[...]
    for a_ref, w_ref in zip(a_refs, w_refs):
        x = x + jnp.dot(a_ref[...], w_ref[...], preferred_element_type=F32)
    h = _rms(x, g_ref[...]).astype(BF16)
    q = jnp.dot(h, wq_ref[...], preferred_element_type=F32)
    heads = []
    for hd in range(XA_HEADS):
        cols = slice(hd * XA_HEAD_DIM, (hd + 1) * XA_HEAD_DIM)
        qh = (_rms(q[:, cols], qn_ref[...]) * XA_HEAD_DIM ** -0.5).astype(BF16)
        s = lax.dot_general(qh, k_ref[:, cols], NT_DIMS, preferred_element_type=F32)
        p = jnp.exp(s - jnp.max(s, axis=-1, keepdims=True))
        p = p * (1.0 / jnp.sum(p, axis=-1, keepdims=True))
        heads.append(jnp.dot(p.astype(BF16), v_ref[:, cols],
                             preferred_element_type=F32).astype(BF16))
    o = jnp.concatenate(heads, axis=1)
    o_ref[...] = x + jnp.dot(o, wo_ref[...], preferred_element_type=F32)


def _cross_attention(x, acts, weights, gain, w_q, q_gain, k, v, w_o, seq, n_mem):
    m, d = x.shape
    tm = min(seq, TOKEN_TILE)
    per_seq = seq // tm
    row = lambda i: (i, 0)
    mem_row = lambda i: (i // per_seq, 0)
    return pl.pallas_call(
        functools.partial(_xa_kernel, len(acts)),
        grid=(m // tm,),
        in_specs=[pl.BlockSpec((tm, d), row)]
        + [pl.BlockSpec((tm, a.shape[1]), row) for a in acts]
        + [_resident(w.shape) for w in weights]
        + [pl.BlockSpec((1, d), lambda i: (0, 0)),
           _resident(w_q.shape), pl.BlockSpec((1, XA_HEAD_DIM), lambda i: (0, 0)),
           pl.BlockSpec((n_mem, d), mem_row), pl.BlockSpec((n_mem, d), mem_row),
           _resident(w_o.shape)],
        out_specs=pl.BlockSpec((tm, d), row),
        out_shape=jax.ShapeDtypeStruct((m, d), F32),
        compiler_params=_params("arbitrary"),
    )(x, *acts, *weights, gain.reshape(1, d), w_q, q_gain.reshape(1, XA_HEAD_DIM), k, v, w_o)


def _ffn_kernel(per_seq, d_ff, x_ref, g_ref, wup_ref, cw_ref, cb_ref, wdn_ref, o_ref,
                halo_ref):
    tm = x_ref.shape[0]
    fc = FF_CHUNK
    n_chunks = d_ff // fc
    groups = tm // 8

    @pl.when(pl.program_id(0) % per_seq == 0)
    def _():
        halo_ref[...] = jnp.zeros_like(halo_ref)

    x = x_ref[...]
    h = _rms(x, g_ref[...]).astype(BF16)
    sub = lax.broadcasted_iota(jnp.int32, (groups, 8, fc), 1)

    def up_proj(j):
        return [jnp.dot(h, wup_ref[:, c0:c0 + fc], preferred_element_type=F32)
                for c0 in (j * fc, d_ff + j * fc)]

    def conv(u, col0, slot):
        cols = slice(col0, col0 + fc)
        prev = halo_ref[slot]
        halo_ref[slot] = u[tm - 8:, :]
        grouped = jnp.concatenate([prev, u], axis=0).reshape(groups + 1, 8, fc)
        out = cw_ref[2:3, cols] * u + cb_ref[:, cols]
        for n in (1, 2):
            rot = pltpu.roll(grouped, n, 1)
            shifted = jnp.where(sub < n, rot[:groups], rot[1:]).reshape(tm, fc)
            out = out + cw_ref[2 - n:3 - n, cols] * shifted
        return out

    acc = x
    nxt = up_proj(0)
    for j in range(n_chunks):
        cur = nxt
        if j + 1 < n_chunks:
            nxt = up_proj(j + 1)
        gate = conv(cur[0], j * fc, j)
        up = conv(cur[1], d_ff + j * fc, n_chunks + j)
        act = (gate * (1.0 / (1.0 + jnp.exp(-gate))) * up).astype(BF16)
        acc = acc + jnp.dot(act, wdn_ref[j * fc:(j + 1) * fc, :], preferred_element_type=F32)
    o_ref[...] = acc


def _conv_ffn(x, gain, w_up, conv_w, conv_b, w_down, seq):
    m, d = x.shape
    d_ff = w_down.shape[0]
    assert d_ff % FF_CHUNK == 0
    tm = min(seq, TOKEN_TILE)
    row = lambda i: (i, 0)
    const = lambda i: (0, 0)
    return pl.pallas_call(
        functools.partial(_ffn_kernel, seq // tm, d_ff),
        grid=(m // tm,),
        in_specs=[pl.BlockSpec((tm, d), row), pl.BlockSpec((1, d), const),
                  _resident(w_up.shape), pl.BlockSpec(conv_w.shape, const),
                  pl.BlockSpec((1, 2 * d_ff), const), _resident(w_down.shape)],
        out_specs=pl.BlockSpec((tm, d), row),
        out_shape=jax.ShapeDtypeStruct((m, d), F32),
        scratch_shapes=[pltpu.VMEM((2 * d_ff // FF_CHUNK, 8, FF_CHUNK), F32)],
        compiler_params=_params("arbitrary"),
    )(x, gain.reshape(1, d), w_up, conv_w, conv_b.reshape(1, 2 * d_ff), w_down)


def kernel(x, mem, positions, norm_mix, norm_cross, norm_mem, norm_ffn, ab_w_in, ab_w_out, swa_q_norm, swa_k_norm, swa_sinks, hgrn_w_in, hgrn_w_out, hgrn_o_norm, hgrn_lb, xa_w_q, xa_w_kv, xa_w_o, xa_q_norm, xa_k_norm, ffn_w_up, ffn_conv_w, ffn_conv_b, ffn_w_down):
    batch, seq, d = x.shape
    n_mem = mem.shape[1]
    depth = norm_mix.shape[0]
    assert d == D_MODEL and seq % TOKEN_TILE == 0
    assert (batch * n_mem) % min(batch * n_mem, TOKEN_TILE) == 0
    bf = lambda w: w.astype(BF16)

    p_lb = jax.nn.softmax(hgrn_lb.astype(F32), axis=0)
    lower_bounds = jnp.cumsum(p_lb, axis=0) - p_lb[0]

    xf = x.reshape(batch * seq, d)
    memf = mem.reshape(batch * n_mem, d)
    cos, sin = _rope_tables(positions)

    for l in range(depth):
        if l % 2 == 0:
            e = l // 2
            qkv = _norm_matmul(xf, norm_mix[l], bf(ab_w_in[e]), tn=AB_IN // 3)
            out_a = _swa_attention(qkv, cos, sin, swa_q_norm[e], swa_k_norm[e], swa_sinks[e],
                                   batch, seq)
            out_b = _sb_attention(qkv, batch, seq)
            w_out = bf(ab_w_out[e])
            mixed, mix_w = [out_a, out_b], [w_out[:SWA_Q_W], w_out[SWA_Q_W:]]
        else:
            o = l // 2
            proj = _norm_matmul(xf, norm_mix[l], bf(hgrn_w_in[o]), tn=D_MODEL)
            mixed = [_hgrn_mixer_core(proj, lower_bounds[l], hgrn_o_norm[o], batch, seq)]
            mix_w = [bf(hgrn_w_out[o])]
        k, v = _xa_kv(memf, norm_mem[l], bf(xa_w_kv[l]), xa_k_norm[l])
        xf = _cross_attention(xf, mixed, mix_w, norm_cross[l], bf(xa_w_q[l]), xa_q_norm[l], k, v,
                              bf(xa_w_o[l]), seq, n_mem)
        xf = _conv_ffn(xf, norm_ffn[l], bf(ffn_w_up[l]), ffn_conv_w[l], ffn_conv_b[l],
                       bf(ffn_w_down[l]), seq)
    return xf.reshape(batch, seq, d)
```

```python
import functools

import jax
import jax.numpy as jnp
from jax import lax
from jax.experimental import pallas as pl
from jax.experimental.pallas import tpu as pltpu

F32 = jnp.float32
BF16 = jnp.bfloat16

D_MODEL = 1024
HEAD_DIM = 64
SWA_HEADS = 8
SWA_KV_HEADS = 2
WINDOW = 128
SB_HEADS = 8
SB_BLOCK = 128
HGRN_HEAD_DIM = 128
HGRN_HEADS = D_MODEL // HGRN_HEAD_DIM
XA_HEADS = 4
XA_HEAD_DIM = D_MODEL // XA_HEADS
ROPE_THETA = 10000.0
EPS = 1e-6
SWA_Q_W = SWA_HEADS * HEAD_DIM
SWA_KV_W = SWA_KV_HEADS * HEAD_DIM
SB_W = SB_HEADS * HEAD_DIM
AB_IN = SWA_Q_W + 2 * SWA_KV_W + 3 * SB_W

LANES = 128
MXU_DIM = 256
TOKEN_TILE = 512
FF_CHUNK = 256
HGRN_CHUNK = 32
HGRN_STATE_CHUNK = 64
HGRN_PRE_TILE = 256
HGRN_HEADS_PER_STEP = 4
HGRN_SAFE_LOG_DECAY = -60.0
SB_Q_TILE = 512
VMEM_LIMIT = 56 * 1024 * 1024

NT_DIMS = (((1,), (1,)), ((), ()))
TN_DIMS = (((0,), (0,)), ((), ()))


def _params(*sem):
    return pltpu.CompilerParams(dimension_semantics=sem, vmem_limit_bytes=VMEM_LIMIT)


def _resident(shape):
    zeros = (0,) * len(shape)
    return pl.BlockSpec(shape, lambda *_: zeros, pipeline_mode=pl.Buffered(1))


def _rms(x, gain):
    ms = jnp.mean(x * x, axis=-1, keepdims=True)
    return x * lax.rsqrt(ms + EPS) * gain


def _split_bf16(x):
    hi = x.astype(BF16)
    lo = (x - hi.astype(F32)).astype(BF16)
    return hi, lo


def _sigmoid_pair(x):
    e = jnp.exp(-jnp.abs(x))
    big = 1.0 / (1.0 + e)
    small = e * big
    pos = x >= 0
    return jnp.where(pos, big, small), jnp.where(pos, small, big)


def _norm_matmul_kernel(tn, x_ref, g_ref, w_ref, o_ref):
    h = _rms(x_ref[...], g_ref[...]).astype(BF16)
    for j in range(w_ref.shape[1] // tn):
        cols = slice(j * tn, (j + 1) * tn)
        o_ref[:, cols] = jnp.dot(h, w_ref[:, cols], preferred_element_type=F32).astype(BF16)


def _norm_matmul(x, gain, w, tn):
    m, d = x.shape
    n = w.shape[1]
    assert n % tn == 0
    tm = TOKEN_TILE
    row = lambda i: (i, 0)
    return pl.pallas_call(
        functools.partial(_norm_matmul_kernel, tn),
        grid=(m // tm,),
        in_specs=[pl.BlockSpec((tm, d), row), pl.BlockSpec((1, d), lambda i: (0, 0)),
                  _resident(w.shape)],
        out_specs=pl.BlockSpec((tm, n), row),
        out_shape=jax.ShapeDtypeStruct((m, n), BF16),
        compiler_params=_params("arbitrary"),
    )(x, gain.reshape(1, d), w)


def _rope_table_kernel(pos_ref, inv_ref, cos_ref, sin_ref):
    ang = pos_ref[...] * inv_ref[...]
    lane = lax.broadcasted_iota(jnp.int32, ang.shape, 1)
    first_half = (lane % HEAD_DIM) < HEAD_DIM // 2
    cos_ref[...] = jnp.cos(ang)
    sin = jnp.sin(ang)
    sin_ref[...] = jnp.where(first_half, -sin, sin)


def _rope_tables(positions):
    m = positions.size
    tm = TOKEN_TILE
    half = HEAD_DIM // 2
    inv_freq = ROPE_THETA ** (-jnp.arange(0, HEAD_DIM, 2, dtype=F32) / HEAD_DIM)
    inv = jnp.tile(inv_freq, LANES // half).reshape(1, LANES)
    pos = positions.astype(F32).reshape(m, 1)
    row = lambda i: (i, 0)
    return pl.pallas_call(
        _rope_table_kernel,
        grid=(m // tm,),
        in_specs=[pl.BlockSpec((tm, 1), row), pl.BlockSpec((1, LANES), lambda i: (0, 0))],
        out_specs=[pl.BlockSpec((tm, LANES), row), pl.BlockSpec((tm, LANES), row)],
        out_shape=[jax.ShapeDtypeStruct((m, LANES), F32)] * 2,
        compiler_params=_params("arbitrary"),
    )(pos, inv)


def _group_ones(width):
    r = lax.broadcasted_iota(jnp.int32, (width, width), 0) // HEAD_DIM
    c = lax.broadcasted_iota(jnp.int32, (width, width), 1) // HEAD_DIM
    return (r == c).astype(BF16)


def _head_norm_rope(x, gain, cos, sin):
    hi, lo = _split_bf16(x * x)
    ones = _group_ones(LANES)
    ss = (jnp.dot(hi, ones, preferred_element_type=F32)
          + jnp.dot(lo, ones, preferred_element_type=F32))
    y = x * lax.rsqrt(ss * (1.0 / HEAD_DIM) + EPS) * gain
    lane = lax.broadcasted_iota(jnp.int32, y.shape, 1)
    first_half = (lane % HEAD_DIM) < HEAD_DIM // 2
    partner = jnp.where(first_half,
                        pltpu.roll(y, LANES - HEAD_DIM // 2, 1),
                        pltpu.roll(y, HEAD_DIM // 2, 1))
    return y * cos + partner * sin


def _swa_kernel(seq, qkv_ref, cos_ref, sin_ref, qg_ref, kg_ref, sink_ref, o_ref,
                q_scr, k_scr, v_scr):
    rt = min(seq, TOKEN_TILE)
    n_q_chunks = SWA_Q_W // LANES
    for r in range(seq // rt):
        rows = slice(r * rt, (r + 1) * rt)
        cos, sin = cos_ref[rows, :], sin_ref[rows, :]
        for c in range(n_q_chunks):
            cols = slice(c * LANES, (c + 1) * LANES)
            xq = qkv_ref[rows, cols].astype(F32)
            q = _head_norm_rope(xq, qg_ref[...], cos, sin)
            q_scr[c, rows, :] = (q * HEAD_DIM ** -0.5).astype(BF16)
        xk = qkv_ref[rows, SWA_Q_W:SWA_Q_W + SWA_KV_W].astype(F32)
        k = _head_norm_rope(xk, kg_ref[...], cos, sin)
        v = qkv_ref[rows, SWA_Q_W + SWA_KV_W:SWA_Q_W + 2 * SWA_KV_W].astype(F32)
        low = lax.broadcasted_iota(jnp.int32, k.shape, 1) < HEAD_DIM
        for t, scr in ((k, k_scr), (v, v_scr)):
            swapped = pltpu.roll(t, HEAD_DIM, 1)
            scr[0, 0, rows, :] = jnp.where(low, t, 0.0).astype(BF16)
            scr[1, 0, rows, :] = jnp.where(low, 0.0, swapped).astype(BF16)
            scr[0, 1, rows, :] = jnp.where(low, swapped, 0.0).astype(BF16)
            scr[1, 1, rows, :] = jnp.where(low, 0.0, t).astype(BF16)

    w = WINDOW
    qi = lax.broadcasted_iota(jnp.int32, (2 * w, w), 0) % w
    kj = lax.broadcasted_iota(jnp.int32, (2 * w, w), 1)
    top = lax.broadcasted_iota(jnp.int32, (2 * w, 1), 0) < w
    cur_mask = kj <= qi

    def block(n, carry):
        r0 = pl.multiple_of(n * w, w)
        p0 = pl.multiple_of(jnp.maximum(n - 1, 0) * w, w)
        prev_mask = (kj > qi) & (n > 0)
        for g in range(SWA_KV_HEADS):
            qa = jnp.concatenate([q_scr[2 * g, pl.ds(r0, w), :],
                                  q_scr[2 * g + 1, pl.ds(r0, w), :]], axis=0)
            out = jnp.zeros((2 * w, LANES), F32)
            for slot in range(2):
                s_cur = lax.dot_general(qa, k_scr[slot, g, pl.ds(r0, w), :], NT_DIMS,
                                        preferred_element_type=F32)
                s_prev = lax.dot_general(qa, k_scr[slot, g, pl.ds(p0, w), :], NT_DIMS,
                                         preferred_element_type=F32)
                s_cur = jnp.where(cur_mask, s_cur, -jnp.inf)
                s_prev = jnp.where(prev_mask, s_prev, -jnp.inf)
                sink = jnp.where(top, sink_ref[4 * g + slot], sink_ref[4 * g + 2 + slot])
                m = jnp.maximum(jnp.max(jnp.maximum(s_cur, s_prev), axis=-1, keepdims=True), sink)
                p_cur = jnp.exp(s_cur - m)
                p_prev = jnp.exp(s_prev - m)
                den = jnp.sum(p_cur + p_prev, axis=-1, keepdims=True) + jnp.exp(sink - m)
                inv = 1.0 / den
                out = out + jnp.dot((p_cur * inv).astype(BF16), v_scr[slot, g, pl.ds(r0, w), :],
                                    preferred_element_type=F32)
                out = out + jnp.dot((p_prev * inv).astype(BF16), v_scr[slot, g, pl.ds(p0, w), :],
                                    preferred_element_type=F32)
            o_ref[pl.ds(r0, w), 2 * g * LANES:(2 * g + 1) * LANES] = out[:w].astype(BF16)
            o_ref[pl.ds(r0, w), (2 * g + 1) * LANES:(2 * g + 2) * LANES] = out[w:].astype(BF16)
        return carry

    lax.fori_loop(0, seq // w, block, 0, unroll=4)


def _swa_attention(qkv, cos, sin, q_gain, k_gain, sinks, batch, seq):
    width = SWA_Q_W + 2 * SWA_KV_W
    assert AB_IN % width == 0
    const = lambda b: (0, 0)
    return pl.pallas_call(
        functools.partial(_swa_kernel, seq),
        grid=(batch,),
        in_specs=[
            pl.BlockSpec((seq, width), lambda b: (b, 0)),
            pl.BlockSpec((seq, LANES), lambda b: (b, 0)),
            pl.BlockSpec((seq, LANES), lambda b: (b, 0)),
            pl.BlockSpec((1, LANES), const),
            pl.BlockSpec((1, LANES), const),
            pl.BlockSpec(memory_space=pltpu.SMEM),
        ],
        out_specs=pl.BlockSpec((seq, SWA_Q_W), lambda b: (b, 0)),
        out_shape=jax.ShapeDtypeStruct((batch * seq, SWA_Q_W), BF16),
        scratch_shapes=[
            pltpu.VMEM((SWA_Q_W // LANES, seq, LANES), BF16),
            pltpu.VMEM((2, SWA_KV_HEADS, seq, LANES), BF16),
            pltpu.VMEM((2, SWA_KV_HEADS, seq, LANES), BF16),
        ],
        compiler_params=_params("arbitrary"),
    )(qkv, cos, sin,
      jnp.tile(q_gain, LANES // HEAD_DIM).reshape(1, LANES),
      jnp.tile(k_gain, LANES // HEAD_DIM).reshape(1, LANES),
      sinks)


def _neg_abs(x):
    bits = lax.bitcast_convert_type(x, jnp.uint32) | jnp.uint32(0x80000000)
    return lax.bitcast_convert_type(bits, F32)


def _sb_kernel(seq, q_ref, k_ref, v_ref, o_ref, k_scr, v_scr, later_scr, causal_scr, acc_ref,
               carry_ref):
    blk = SB_BLOCK
    tq = min(seq, SB_Q_TILE)
    per_tile = tq // blk
    nb = seq // blk
    low = lax.broadcasted_iota(jnp.int32, (seq, LANES), 1) < HEAD_DIM
    for src, scr in ((k_ref, k_scr), (v_ref, v_scr)):
        full = src[...]
        zero = jnp.zeros_like(full)
        scr[:, 0:blk, :] = jnp.where(low, full, zero).reshape(nb, blk, LANES)
        scr[:, blk:2 * blk, :] = jnp.where(low, zero, full).reshape(nb, blk, LANES)

    r = lax.broadcasted_iota(jnp.int32, (2 * blk, 2 * blk), 0)
    c = lax.broadcasted_iota(jnp.int32, (2 * blk, 2 * blk), 1)
    later_scr[...] = ((r // blk == c // blk) & (r % blk > c % blk)).astype(BF16)
    causal_scr[...] = (lax.broadcasted_iota(jnp.int32, (tq, 2 * blk), 1) % blk
                       < lax.broadcasted_iota(jnp.int32, (tq, 2 * blk), 0)).astype(F32)

    def scores(q_rows, n_rows, kb, mask):
        qt = q_ref[pl.ds(q_rows, n_rows), :] * jnp.asarray(HEAD_DIM ** -0.5, BF16)
        z = lax.dot_general(qt, k_scr[kb], NT_DIMS, preferred_element_type=F32)
        log1p_e = jnp.log(1.0 + jnp.exp(_neg_abs(z)))
        log_beta = jnp.minimum(z, 0.0) - log1p_e
        log_fail = log_beta - z
        if mask is not None:
            log_fail = log_fail * mask
        after = jnp.dot(log_fail.astype(BF16), later_scr[...], preferred_element_type=F32)
        totals = [after[:, h * blk:h * blk + 1] + log_fail[:, h * blk:h * blk + 1]
                  for h in range(2)]
        return log_beta + after, totals

    def weights(parts, carry, mask):
        log_w, totals = parts
        a = jnp.exp(log_w + carry)
        if mask is not None:
            a = a * mask
        halves = [carry[:, h * blk:(h + 1) * blk] + totals[h] for h in range(2)]
        return a.astype(BF16), jnp.concatenate(halves, axis=1)

    def tile(qi, carry_):
        row0 = pl.multiple_of(qi * tq, tq)
        first = qi * per_tile
        acc_ref[...] = jnp.zeros_like(acc_ref)
        carry_ref[...] = jnp.zeros_like(carry_ref)
        for j in reversed(range(per_tile)):
            n_rows = tq - j * blk
            rows = pl.ds(j * blk, n_rows)
            mask = causal_scr[0:n_rows, :]
            a, carry = weights(scores(row0 + j * blk, n_rows, first + j, mask),
                               carry_ref[rows, :], mask)
            acc_ref[rows, :] += jnp.dot(a, v_scr[first + j], preferred_element_type=F32)
            carry_ref[rows, :] = carry

        def below(i, c_):
            kbs = [first - 1 - (per_tile * i + u) for u in range(per_tile)]
            parts = [scores(row0, tq, kb, None) for kb in kbs]
            carry = carry_ref[...]
            pv = jnp.zeros((tq, LANES), F32)
            for kb, p in zip(kbs, parts):
                a, carry = weights(p, carry, None)
                pv = pv + jnp.dot(a, v_scr[kb], preferred_element_type=F32)
            acc_ref[...] += pv
            carry_ref[...] = carry
            return c_

        lax.fori_loop(0, qi, below, 0)
        o_ref[pl.ds(row0, tq), :] = acc_ref[...].astype(BF16)
        return carry_

    lax.fori_loop(0, seq // tq, tile, 0)


def _sb_attention(qkv, batch, seq):
    tq = min(seq, SB_Q_TILE)
    pairs = SB_W // LANES
    q0 = (SWA_Q_W + 2 * SWA_KV_W) // LANES
    k0, v0 = q0 + pairs, q0 + 2 * pairs
    return pl.pallas_call(
        functools.partial(_sb_kernel, seq),
        grid=(batch, pairs),
        in_specs=[
            pl.BlockSpec((seq, LANES), lambda b, p: (b, q0 + p)),
            pl.BlockSpec((seq, LANES), lambda b, p: (b, k0 + p)),
            pl.BlockSpec((seq, LANES), lambda b, p: (b, v0 + p)),
        ],
        out_specs=pl.BlockSpec((seq, LANES), lambda b, p: (b, p)),
        out_shape=jax.ShapeDtypeStruct((batch * seq, SB_W), BF16),
        scratch_shapes=[
            pltpu.VMEM((seq // SB_BLOCK, 2 * SB_BLOCK, LANES), BF16),
            pltpu.VMEM((seq // SB_BLOCK, 2 * SB_BLOCK, LANES), BF16),
            pltpu.VMEM((2 * SB_BLOCK, 2 * SB_BLOCK), BF16),
            pltpu.VMEM((tq, 2 * SB_BLOCK), F32),
            pltpu.VMEM((tq, LANES), F32),
            pltpu.VMEM((tq, 2 * SB_BLOCK), F32),
        ],
        compiler_params=_params("arbitrary", "arbitrary"),
    )(qkv, qkv, qkv)


def _hgrn_kernel(seq, q_ref, f_ref, i_ref, g_ref, lb_ref, on_ref, o_ref,
                 qs_scr, ks_scr, b_scr, o_scr, dec_scr, st_scr, qrow_scr):
    ch = HGRN_CHUNK
    sc = HGRN_STATE_CHUNK
    hd = HGRN_HEAD_DIM
    heads = HGRN_HEADS_PER_STEP
    width = heads * hd
    pt = min(seq, HGRN_PRE_TILE)

    r = lax.broadcasted_iota(jnp.int32, (pt, pt), 0)
    c = lax.broadcasted_iota(jnp.int32, (pt, pt), 1)
    same_sub = (r // ch) == (c // ch)
    same_chunk = (r // sc) == (c // sc)
    m_diag = same_sub & (c <= r)
    m_prev = same_chunk & ((r // ch) == (c // ch) + 1)
    twice = lambda m: jnp.concatenate([m.astype(BF16)] * 2, axis=1)
    cum_sub, tot_sub, tot_chunk = twice(m_diag), twice(same_sub), twice(same_chunk)
    first_sub = (lax.broadcasted_iota(jnp.int32, (pt, 1), 0) // ch) % (sc // ch) == 0
    s_idx = lax.broadcasted_iota(jnp.int32, (ch, 1), 0)
    row8 = lax.broadcasted_iota(jnp.int32, (8, 1), 0)
    lb = lb_ref[...]

    def exact_diagonal(row0):
        def sub_chunk(n, carry):
            r0 = pl.multiple_of(row0 + n * ch, ch)
            for h in range(heads):
                cols = slice(h * hd, (h + 1) * hd)
                b_c = b_scr[pl.ds(r0, ch), cols]
                _, sig_neg = _sigmoid_pair(f_ref[pl.ds(r0, ch), cols].astype(F32))
                k_c = (1.0 - lb[:, cols]) * sig_neg
                v_c = i_ref[pl.ds(r0, ch), cols].astype(F32)
                qrow_scr[...] = q_ref[pl.ds(r0, ch), cols].astype(F32)

                def row(t, cc, b_c=b_c, k_c=k_c, v_c=v_c, cols=cols):
                    g0 = pl.multiple_of((t // 8) * 8, 8)
                    pick = row8 == (t % 8)
                    take = lambda blk: jnp.sum(jnp.where(pick, blk, 0.0), axis=0, keepdims=True)
                    b_t = take(b_scr[pl.ds(r0 + g0, 8), cols])
                    q_t = take(qrow_scr[pl.ds(g0, 8), :])
                    w = jnp.where(s_idx <= t, jnp.exp(jnp.minimum(b_t - b_c, 0.0)), 0.0)
                    sc_t = jnp.sum(q_t * k_c * w, axis=-1, keepdims=True)
                    o_t = jnp.sum(sc_t * v_c, axis=0, keepdims=True)
                    o_scr[pl.ds(r0 + g0, 8), cols] += jnp.where(pick, o_t, 0.0)
                    return cc

                lax.fori_loop(0, ch, row, 0)
            return carry

        lax.fori_loop(0, pt // ch, sub_chunk, 0)

    def gates(t):
        rows = slice(t * pt, (t + 1) * pt)
        sig, sig_neg = _sigmoid_pair(f_ref[rows, :].astype(F32))
        hi, lo = _split_bf16(jnp.log(lb + (1.0 - lb) * sig))
        return (1.0 - lb) * sig_neg, jnp.concatenate([hi, lo], axis=0)

    def decays(parts):
        return [jnp.dot(m, parts, preferred_element_type=F32)
                for m in (cum_sub, tot_sub, tot_chunk)]

    def factors(t, kk, b, end_sub, end_chunk):
        rows = slice(t * pt, (t + 1) * pt)
        before = jnp.where(first_sub, 0.0, end_chunk - end_sub)
        qd = q_ref[rows, :].astype(F32) * jnp.exp(b)
        qs_scr[rows, :] = (qd * jnp.exp(before)).astype(BF16)
        ks_scr[rows, :] = (kk * jnp.exp(end_chunk - b - before)).astype(BF16)
        b_scr[rows, :] = b
        for i in range(pt // sc):
            dec_scr[t * (pt // sc) + i] = jnp.broadcast_to(
                jnp.exp(end_chunk[i * sc:i * sc + 1, :]), (8, width))
        safe = jnp.min(end_sub) > HGRN_SAFE_LOG_DECAY
        return (qd.astype(BF16), (kk * jnp.exp(-b)).astype(BF16),
                (kk * jnp.exp(end_sub - b)).astype(BF16), safe)

    def intra(t, qd_b, kd_b, ke_b, safe):
        rows = slice(t * pt, (t + 1) * pt)
        v = i_ref[rows, :]
        for h in range(heads):
            cols = slice(h * hd, (h + 1) * hd)
            a_diag = lax.dot_general(qd_b[:, cols], kd_b[:, cols], NT_DIMS,
                                     preferred_element_type=F32)
            a_prev = lax.dot_general(qd_b[:, cols], ke_b[:, cols], NT_DIMS,
                                     preferred_element_type=F32)
            a = jnp.where(m_diag, jnp.where(safe, a_diag, 0.0), jnp.where(m_prev, a_prev, 0.0))
            o_scr[rows, cols] = jnp.dot(a.astype(BF16), v[:, cols], preferred_element_type=F32)

    n_tiles = seq // pt
    g_out, d_out, f_out = {}, {}, {}
    for step in range(n_tiles + 3):
        if step < n_tiles:
            g_out[step] = gates(step)
        if 0 <= step - 1 < n_tiles:
            d_out[step - 1] = decays(g_out[step - 1][1])
        if 0 <= step - 2 < n_tiles:
            f_out[step - 2] = factors(step - 2, g_out[step - 2][0], *d_out[step - 2])
        if 0 <= step - 3 < n_tiles:
            intra(step - 3, *f_out[step - 3])

    for t in range(n_tiles):
        @pl.when(jnp.logical_not(f_out[t][3]))
        def _(t=t):
            exact_diagonal(t * pt)

    st_scr[...] = jnp.zeros_like(st_scr)

    def state_step(n, carry):
        r0 = pl.multiple_of(n * sc, sc)
        decay = dec_scr[n]
        for h in range(heads):
            cols = slice(h * hd, (h + 1) * hd)
            st = st_scr[h]
            o_scr[pl.ds(r0, sc), cols] += lax.dot_general(
                qs_scr[pl.ds(r0, sc), cols], st.astype(BF16), NT_DIMS, preferred_element_type=F32)
            kv = lax.dot_general(i_ref[pl.ds(r0, sc), cols], ks_scr[pl.ds(r0, sc), cols], TN_DIMS,
                                 preferred_element_type=F32)
            st_scr[h] = st * decay[0:1, cols] + kv
        return carry

    lax.fori_loop(0, seq // sc, state_step, 0, unroll=4)

    for t in range(seq // pt):
        rows = slice(t * pt, (t + 1) * pt)
        gate = g_ref[rows, :].astype(F32)
        silu = gate * (1.0 / (1.0 + jnp.exp(-gate)))
        for h in range(heads):
            cols = slice(h * hd, (h + 1) * hd)
            y = _rms(o_scr[rows, cols], on_ref[...])
            o_ref[rows, cols] = (y * silu[:, cols]).astype(BF16)


def _hgrn_mixer_core(proj, lower_bound, o_gain, batch, seq):
    heads = HGRN_HEADS_PER_STEP
    width = heads * HGRN_HEAD_DIM
    groups = HGRN_HEADS // heads
    assert seq % HGRN_PRE_TILE == 0 and HGRN_PRE_TILE % HGRN_STATE_CHUNK == 0
    section = lambda k: pl.BlockSpec((seq, width), lambda b, g, k=k: (b, k * groups + g))
    return pl.pallas_call(
        functools.partial(_hgrn_kernel, seq),
        grid=(batch, groups),
        in_specs=[section(0), section(1), section(2), section(3),
                  pl.BlockSpec((1, width), lambda b, g: (0, g)),
                  pl.BlockSpec((1, HGRN_HEAD_DIM), lambda b, g: (0, 0))],
        out_specs=pl.BlockSpec((seq, width), lambda b, g: (b, g)),
        out_shape=jax.ShapeDtypeStruct((batch * seq, D_MODEL), BF16),
        scratch_shapes=[
            pltpu.VMEM((seq, width), BF16),
            pltpu.VMEM((seq, width), BF16),
            pltpu.VMEM((seq, width), F32),
            pltpu.VMEM((seq, width), F32),
            pltpu.VMEM((seq // HGRN_STATE_CHUNK, 8, width), F32),
            pltpu.VMEM((heads, HGRN_HEAD_DIM, HGRN_HEAD_DIM), F32),
            pltpu.VMEM((HGRN_CHUNK, HGRN_HEAD_DIM), F32),
        ],
        compiler_params=_params("arbitrary", "arbitrary"),
    )(proj, proj, proj, proj, lower_bound.reshape(1, D_MODEL), o_gain.reshape(1, HGRN_HEAD_DIM))


def _xa_kv_kernel(mem_ref, g_ref, w_ref, kn_ref, k_ref, v_ref):
    mem_n = _rms(mem_ref[...], g_ref[...]).astype(BF16)
    kv = jnp.dot(mem_n, w_ref[...], preferred_element_type=F32)
    for h in range(XA_HEADS):
        cols = slice(h * XA_HEAD_DIM, (h + 1) * XA_HEAD_DIM)
        k_ref[:, cols] = _rms(kv[:, cols], kn_ref[...]).astype(BF16)
    v_ref[...] = kv[:, D_MODEL:].astype(BF16)


def _xa_kv(mem, gain, w_kv, k_gain):
    m, d = mem.shape
    tm = min(m, TOKEN_TILE)
    row = lambda i: (i, 0)
    return pl.pallas_call(
        _xa_kv_kernel,
        grid=(m // tm,),
        in_specs=[pl.BlockSpec((tm, d), row), pl.BlockSpec((1, d), lambda i: (0, 0)),
                  _resident(w_kv.shape), pl.BlockSpec((1, XA_HEAD_DIM), lambda i: (0, 0))],
        out_specs=[pl.BlockSpec((tm, d), row), pl.BlockSpec((tm, d), row)],
        out_shape=[jax.ShapeDtypeStruct((m, d), BF16)] * 2,
        compiler_params=_params("arbitrary"),
    )(mem, gain.reshape(1, d), w_kv, k_gain.reshape(1, XA_HEAD_DIM))


def _xa_kernel(n_in, x_ref, *refs):
    a_refs, w_refs = refs[:n_in], refs[n_in:2 * n_in]
    g_ref, wq_ref, qn_ref, k_ref, v_ref, wo_ref, o_ref = refs[2 * n_in:]
    x = x_ref[...]
    for a_ref, w_ref in zip(a_refs, w_refs):
        x = x + jnp.dot(a_ref[...], w_ref[...], preferred_element_type=F32)
    h = _rms(x, g_ref[...]).astype(BF16)
    q = jnp.dot(h, wq_ref[...], preferred_element_type=F32)
    heads = []
    for hd in range(XA_HEADS):
        cols = slice(hd * XA_HEAD_DIM, (hd + 1) * XA_HEAD_DIM)
        qh = (_rms(q[:, cols], qn_ref[...]) * XA_HEAD_DIM ** -0.5).astype(BF16)
        s = lax.dot_general(qh, k_ref[:, cols], NT_DIMS, preferred_element_type=F32)
        p = jnp.exp(s - jnp.max(s, axis=-1, keepdims=True))
        p = p * (1.0 / jnp.sum(p, axis=-1, keepdims=True))
        heads.append(jnp.dot(p.astype(BF16), v_ref[:, cols],
                             preferred_element_type=F32).astype(BF16))
    o = jnp.concatenate(heads, axis=1)
    o_ref[...] = x + jnp.dot(o, wo_ref[...], preferred_element_type=F32)


def _cross_attention(x, acts, weights, gain, w_q, q_gain, k, v, w_o, seq, n_mem):
    m, d = x.shape
    tm = min(seq, TOKEN_TILE)
    per_seq = seq // tm
    row = lambda i: (i, 0)
    mem_row = lambda i: (i // per_seq, 0)
    return pl.pallas_call(
        functools.partial(_xa_kernel, len(acts)),
        grid=(m // tm,),
        in_specs=[pl.BlockSpec((tm, d), row)]
        + [pl.BlockSpec((tm, a.shape[1]), row) for a in acts]
        + [_resident(w.shape) for w in weights]
        + [pl.BlockSpec((1, d), lambda i: (0, 0)),
           _resident(w_q.shape), pl.BlockSpec((1, XA_HEAD_DIM), lambda i: (0, 0)),
           pl.BlockSpec((n_mem, d), mem_row), pl.BlockSpec((n_mem, d), mem_row),
           _resident(w_o.shape)],
        out_specs=pl.BlockSpec((tm, d), row),
        out_shape=jax.ShapeDtypeStruct((m, d), F32),
        compiler_params=_params("arbitrary"),
    )(x, *acts, *weights, gain.reshape(1, d), w_q, q_gain.reshape(1, XA_HEAD_DIM), k, v, w_o)


def _ffn_kernel(per_seq, d_ff, x_ref, g_ref, wup_ref, cw_ref, cb_ref, wdn_ref, o_ref,
                halo_ref):
    tm = x_ref.shape[0]
    fc = FF_CHUNK
    n_chunks = d_ff // fc
    groups = tm // 8

    @pl.when(pl.program_id(0) % per_seq == 0)
    def _():
        halo_ref[...] = jnp.zeros_like(halo_ref)

    x = x_ref[...]
    h = _rms(x, g_ref[...]).astype(BF16)
    sub = lax.broadcasted_iota(jnp.int32, (groups, 8, fc), 1)


    def up_proj(j):
        return [jnp.dot(h, wup_ref[:, c0:c0 + fc], preferred_element_type=F32)
                for c0 in (j * fc, d_ff + j * fc)]

    def conv(u, col0, slot):
        cols = slice(col0, col0 + fc)
        prev = halo_ref[slot]
        halo_ref[slot] = u[tm - 8:, :]
        grouped = jnp.concatenate([prev, u], axis=0).reshape(groups + 1, 8, fc)
        out = cw_ref[2:3, cols] * u + cb_ref[:, cols]
        for n in (1, 2):
            rot = pltpu.roll(grouped, n, 1)
            shifted = jnp.where(sub < n, rot[:groups], rot[1:]).reshape(tm, fc)
            out = out + cw_ref[2 - n:3 - n, cols] * shifted
        return out

    nxt = up_proj(0)
    acts = []
    for j in range(n_chunks):
        cur = nxt
        if j + 1 < n_chunks:
            nxt = up_proj(j + 1)
        gate = conv(cur[0], j * fc, j)
        up = conv(cur[1], d_ff + j * fc, n_chunks + j)
        acts.append((gate * (1.0 / (1.0 + jnp.exp(-gate))) * up).astype(BF16))
    o_ref[...] = x + jnp.dot(jnp.concatenate(acts, axis=1), wdn_ref[...],
                             preferred_element_type=F32)


def _conv_ffn(x, gain, w_up, conv_w, conv_b, w_down, seq):
    m, d = x.shape
    d_ff = w_down.shape[0]
    assert d_ff % FF_CHUNK == 0
    tm = min(seq, TOKEN_TILE)
    row = lambda i: (i, 0)
    const = lambda i: (0, 0)
    return pl.pallas_call(
        functools.partial(_ffn_kernel, seq // tm, d_ff),
        grid=(m // tm,),
        in_specs=[pl.BlockSpec((tm, d), row), pl.BlockSpec((1, d), const),
                  _resident(w_up.shape), pl.BlockSpec(conv_w.shape, const),
                  pl.BlockSpec((1, 2 * d_ff), const), _resident(w_down.shape)],
        out_specs=pl.BlockSpec((tm, d), row),
        out_shape=jax.ShapeDtypeStruct((m, d), F32),
        scratch_shapes=[pltpu.VMEM((2 * d_ff // FF_CHUNK, 8, FF_CHUNK), F32)],
        compiler_params=_params("arbitrary"),
    )(x, gain.reshape(1, d), w_up, conv_w, conv_b.reshape(1, 2 * d_ff), w_down)


def kernel(x, mem, positions, norm_mix, norm_cross, norm_mem, norm_ffn, ab_w_in, ab_w_out, swa_q_norm, swa_k_norm, swa_sinks, hgrn_w_in, hgrn_w_out, hgrn_o_norm, hgrn_lb, xa_w_q, xa_w_kv, xa_w_o, xa_q_norm, xa_k_norm, ffn_w_up, ffn_conv_w, ffn_conv_b, ffn_w_down):
    batch, seq, d = x.shape
    n_mem = mem.shape[1]
    depth = norm_mix.shape[0]
    assert d == D_MODEL and seq % TOKEN_TILE == 0
    assert (batch * n_mem) % min(batch * n_mem, TOKEN_TILE) == 0
    bf = lambda w: w.astype(BF16)

    p_lb = jax.nn.softmax(hgrn_lb.astype(F32), axis=0)
    lower_bounds = jnp.cumsum(p_lb, axis=0) - p_lb[0]

    xf = x.reshape(batch * seq, d)
    memf = mem.reshape(batch * n_mem, d)
    cos, sin = _rope_tables(positions)

    for l in range(depth):
        if l % 2 == 0:
            e = l // 2
            qkv = _norm_matmul(xf, norm_mix[l], bf(ab_w_in[e]), tn=AB_IN // 3)
            out_a = _swa_attention(qkv, cos, sin, swa_q_norm[e], swa_k_norm[e], swa_sinks[e],
                                   batch, seq)
            out_b = _sb_attention(qkv, batch, seq)
            w_out = bf(ab_w_out[e])
            mixed, mix_w = [out_a, out_b], [w_out[:SWA_Q_W], w_out[SWA_Q_W:]]
        else:
            o = l // 2
            proj = _norm_matmul(xf, norm_mix[l], bf(hgrn_w_in[o]), tn=D_MODEL)
            mixed = [_hgrn_mixer_core(proj, lower_bounds[l], hgrn_o_norm[o], batch, seq)]
            mix_w = [bf(hgrn_w_out[o])]
        k, v = _xa_kv(memf, norm_mem[l], bf(xa_w_kv[l]), xa_k_norm[l])
        xf = _cross_attention(xf, mixed, mix_w, norm_cross[l], bf(xa_w_q[l]), xa_q_norm[l], k, v,
                              bf(xa_w_o[l]), seq, n_mem)
        xf = _conv_ffn(xf, norm_ffn[l], bf(ffn_w_up[l]), ffn_conv_w[l], ffn_conv_b[l],
                       bf(ffn_w_down[l]), seq)
    return xf.reshape(batch, seq, d)
```

```python
import functools

import jax
import jax.numpy as jnp
from jax import lax
from jax.experimental import pallas as pl
from jax.experimental.pallas import tpu as pltpu

F32 = jnp.float32
BF16 = jnp.bfloat16

D_MODEL = 1024
HEAD_DIM = 64
SWA_HEADS = 8
SWA_KV_HEADS = 2
WINDOW = 128
SB_HEADS = 8
SB_BLOCK = 128
HGRN_HEAD_DIM = 128
HGRN_HEADS = D_MODEL // HGRN_HEAD_DIM
XA_HEADS = 4
XA_HEAD_DIM = D_MODEL // XA_HEADS
ROPE_THETA = 10000.0
EPS = 1e-6
SWA_Q_W = SWA_HEADS * HEAD_DIM
SWA_KV_W = SWA_KV_HEADS * HEAD_DIM
SB_W = SB_HEADS * HEAD_DIM
AB_IN = SWA_Q_W + 2 * SWA_KV_W + 3 * SB_W

LANES = 128
MXU_DIM = 256
TOKEN_TILE = 512
FF_CHUNK = 256
HGRN_CHUNK = 32
HGRN_STATE_CHUNK = 64
HGRN_PRE_TILE = 256
HGRN_HEADS_PER_STEP = 4
HGRN_SAFE_LOG_DECAY = -60.0
SB_Q_TILE = 512
SB_GROUP = 2
SB_EXP_ZERO = -105.0
VMEM_LIMIT = 56 * 1024 * 1024

NT_DIMS = (((1,), (1,)), ((), ()))
TN_DIMS = (((0,), (0,)), ((), ()))


def _params(*sem):
    return pltpu.CompilerParams(dimension_semantics=sem, vmem_limit_bytes=VMEM_LIMIT)


def _resident(shape):
    zeros = (0,) * len(shape)
    return pl.BlockSpec(shape, lambda *_: zeros, pipeline_mode=pl.Buffered(1))


def _rms(x, gain):
    ms = jnp.mean(x * x, axis=-1, keepdims=True)
    return x * lax.rsqrt(ms + EPS) * gain


def _split_bf16(x):
    hi = x.astype(BF16)
    lo = (x - hi.astype(F32)).astype(BF16)
    return hi, lo


def _sigmoid_pair(x):
    e = jnp.exp(-jnp.abs(x))
    big = 1.0 / (1.0 + e)
    small = e * big
    pos = x >= 0
    return jnp.where(pos, big, small), jnp.where(pos, small, big)


def _norm_matmul_kernel(tn, x_ref, g_ref, w_ref, o_ref):
    h = _rms(x_ref[...], g_ref[...]).astype(BF16)
    for j in range(w_ref.shape[1] // tn):
        cols = slice(j * tn, (j + 1) * tn)
        o_ref[:, cols] = jnp.dot(h, w_ref[:, cols], preferred_element_type=F32).astype(BF16)


def _norm_matmul(x, gain, w, tn):
    m, d = x.shape
    n = w.shape[1]
    assert n % tn == 0
    tm = TOKEN_TILE
    row = lambda i: (i, 0)
    return pl.pallas_call(
        functools.partial(_norm_matmul_kernel, tn),
        grid=(m // tm,),
        in_specs=[pl.BlockSpec((tm, d), row), pl.BlockSpec((1, d), lambda i: (0, 0)),
                  _resident(w.shape)],
        out_specs=pl.BlockSpec((tm, n), row),
        out_shape=jax.ShapeDtypeStruct((m, n), BF16),
        compiler_params=_params("arbitrary"),
    )(x, gain.reshape(1, d), w)


def _rope_table_kernel(pos_ref, inv_ref, cos_ref, sin_ref):
    ang = pos_ref[...] * inv_ref[...]
    lane = lax.broadcasted_iota(jnp.int32, ang.shape, 1)
    first_half = (lane % HEAD_DIM) < HEAD_DIM // 2
    cos_ref[...] = jnp.cos(ang)
    sin = jnp.sin(ang)
    sin_ref[...] = jnp.where(first_half, -sin, sin)


def _rope_tables(positions):
    m = positions.size
    tm = TOKEN_TILE
    half = HEAD_DIM // 2
    inv_freq = ROPE_THETA ** (-jnp.arange(0, HEAD_DIM, 2, dtype=F32) / HEAD_DIM)
    inv = jnp.tile(inv_freq, LANES // half).reshape(1, LANES)
    pos = positions.astype(F32).reshape(m, 1)
    row = lambda i: (i, 0)
    return pl.pallas_call(
        _rope_table_kernel,
        grid=(m // tm,),
        in_specs=[pl.BlockSpec((tm, 1), row), pl.BlockSpec((1, LANES), lambda i: (0, 0))],
        out_specs=[pl.BlockSpec((tm, LANES), row), pl.BlockSpec((tm, LANES), row)],
        out_shape=[jax.ShapeDtypeStruct((m, LANES), F32)] * 2,
        compiler_params=_params("arbitrary"),
    )(pos, inv)


def _group_ones(width):
    r = lax.broadcasted_iota(jnp.int32, (width, width), 0) // HEAD_DIM
    c = lax.broadcasted_iota(jnp.int32, (width, width), 1) // HEAD_DIM
    return (r == c).astype(BF16)


def _head_norm_rope(x, gain, cos, sin):
    hi, lo = _split_bf16(x * x)
    ones = _group_ones(LANES)
    ss = (jnp.dot(hi, ones, preferred_element_type=F32)
          + jnp.dot(lo, ones, preferred_element_type=F32))
    y = x * lax.rsqrt(ss * (1.0 / HEAD_DIM) + EPS) * gain
    lane = lax.broadcasted_iota(jnp.int32, y.shape, 1)
    first_half = (lane % HEAD_DIM) < HEAD_DIM // 2
    partner = jnp.where(first_half,
                        pltpu.roll(y, LANES - HEAD_DIM // 2, 1),
                        pltpu.roll(y, HEAD_DIM // 2, 1))
    return y * cos + partner * sin


def _swa_kernel(seq, qkv_ref, cos_ref, sin_ref, qg_ref, kg_ref, sink_ref, o_ref,
                q_scr, k_scr, v_scr):
    rt = min(seq, TOKEN_TILE)
    n_q_chunks = SWA_Q_W // LANES
    for r in range(seq // rt):
        rows = slice(r * rt, (r + 1) * rt)
        cos, sin = cos_ref[rows, :], sin_ref[rows, :]
        for c in range(n_q_chunks):
            cols = slice(c * LANES, (c + 1) * LANES)
            xq = qkv_ref[rows, cols].astype(F32)
            q = _head_norm_rope(xq, qg_ref[...], cos, sin)
            q_scr[c, rows, :] = (q * HEAD_DIM ** -0.5).astype(BF16)
        xk = qkv_ref[rows, SWA_Q_W:SWA_Q_W + SWA_KV_W].astype(F32)
        k = _head_norm_rope(xk, kg_ref[...], cos, sin)
        v = qkv_ref[rows, SWA_Q_W + SWA_KV_W:SWA_Q_W + 2 * SWA_KV_W].astype(F32)
        low = lax.broadcasted_iota(jnp.int32, k.shape, 1) < HEAD_DIM
        for t, scr in ((k, k_scr), (v, v_scr)):
            swapped = pltpu.roll(t, HEAD_DIM, 1)
            scr[0, 0, rows, :] = jnp.where(low, t, 0.0).astype(BF16)
            scr[1, 0, rows, :] = jnp.where(low, 0.0, swapped).astype(BF16)
            scr[0, 1, rows, :] = jnp.where(low, swapped, 0.0).astype(BF16)
            scr[1, 1, rows, :] = jnp.where(low, 0.0, t).astype(BF16)

    w = WINDOW
    qi = lax.broadcasted_iota(jnp.int32, (2 * w, w), 0) % w
    kj = lax.broadcasted_iota(jnp.int32, (2 * w, w), 1)
    top = lax.broadcasted_iota(jnp.int32, (2 * w, 1), 0) < w
    cur_mask = kj <= qi

    def block(n, carry):
        r0 = pl.multiple_of(n * w, w)
        p0 = pl.multiple_of(jnp.maximum(n - 1, 0) * w, w)
        prev_mask = (kj > qi) & (n > 0)
        for g in range(SWA_KV_HEADS):
            qa = jnp.concatenate([q_scr[2 * g, pl.ds(r0, w), :],
                                  q_scr[2 * g + 1, pl.ds(r0, w), :]], axis=0)
            out = jnp.zeros((2 * w, LANES), F32)
            for slot in range(2):
                s_cur = lax.dot_general(qa, k_scr[slot, g, pl.ds(r0, w), :], NT_DIMS,
                                        preferred_element_type=F32)
                s_prev = lax.dot_general(qa, k_scr[slot, g, pl.ds(p0, w), :], NT_DIMS,
                                         preferred_element_type=F32)
                s_cur = jnp.where(cur_mask, s_cur, -jnp.inf)
                s_prev = jnp.where(prev_mask, s_prev, -jnp.inf)
                sink = jnp.where(top, sink_ref[4 * g + slot], sink_ref[4 * g + 2 + slot])
                m = jnp.maximum(jnp.max(jnp.maximum(s_cur, s_prev), axis=-1, keepdims=True), sink)
                p_cur = jnp.exp(s_cur - m)
                p_prev = jnp.exp(s_prev - m)
                den = jnp.sum(p_cur + p_prev, axis=-1, keepdims=True) + jnp.exp(sink - m)
                inv = 1.0 / den
                out = out + jnp.dot((p_cur * inv).astype(BF16), v_scr[slot, g, pl.ds(r0, w), :],
                                    preferred_element_type=F32)
                out = out + jnp.dot((p_prev * inv).astype(BF16), v_scr[slot, g, pl.ds(p0, w), :],
                                    preferred_element_type=F32)
            o_ref[pl.ds(r0, w), 2 * g * LANES:(2 * g + 1) * LANES] = out[:w].astype(BF16)
            o_ref[pl.ds(r0, w), (2 * g + 1) * LANES:(2 * g + 2) * LANES] = out[w:].astype(BF16)
        return carry

    lax.fori_loop(0, seq // w, block, 0, unroll=4)


def _swa_attention(qkv, cos, sin, q_gain, k_gain, sinks, batch, seq):
    width = SWA_Q_W + 2 * SWA_KV_W
    assert AB_IN % width == 0
    const = lambda b: (0, 0)
    return pl.pallas_call(
        functools.partial(_swa_kernel, seq),
        grid=(batch,),
        in_specs=[
            pl.BlockSpec((seq, width), lambda b: (b, 0)),
            pl.BlockSpec((seq, LANES), lambda b: (b, 0)),
            pl.BlockSpec((seq, LANES), lambda b: (b, 0)),
            pl.BlockSpec((1, LANES), const),
            pl.BlockSpec((1, LANES), const),
            pl.BlockSpec(memory_space=pltpu.SMEM),
        ],
        out_specs=pl.BlockSpec((seq, SWA_Q_W), lambda b: (b, 0)),
        out_shape=jax.ShapeDtypeStruct((batch * seq, SWA_Q_W), BF16),
        scratch_shapes=[
            pltpu.VMEM((SWA_Q_W // LANES, seq, LANES), BF16),
            pltpu.VMEM((2, SWA_KV_HEADS, seq, LANES), BF16),
            pltpu.VMEM((2, SWA_KV_HEADS, seq, LANES), BF16),
        ],
        compiler_params=_params("arbitrary"),
    )(qkv, cos, sin,
      jnp.tile(q_gain, LANES // HEAD_DIM).reshape(1, LANES),
      jnp.tile(k_gain, LANES // HEAD_DIM).reshape(1, LANES),
      sinks)


def _neg_abs(x):
    bits = lax.bitcast_convert_type(x, jnp.uint32) | jnp.uint32(0x80000000)
    return lax.bitcast_convert_type(bits, F32)


def _sb_kernel(seq, q_ref, k_ref, v_ref, o_ref, k_scr, v_scr, later_scr, causal_scr, acc_ref,
               carry_ref):
    blk = SB_BLOCK
    tq = min(seq, SB_Q_TILE)
    per_tile = tq // blk
    nb = seq // blk
    low = lax.broadcasted_iota(jnp.int32, (seq, LANES), 1) < HEAD_DIM
    for src, scr in ((k_ref, k_scr), (v_ref, v_scr)):
        full = src[...]
        zero = jnp.zeros_like(full)
        scr[:, 0:blk, :] = jnp.where(low, full, zero).reshape(nb, blk, LANES)
        scr[:, blk:2 * blk, :] = jnp.where(low, zero, full).reshape(nb, blk, LANES)

    r = lax.broadcasted_iota(jnp.int32, (2 * blk, 2 * blk), 0)
    c = lax.broadcasted_iota(jnp.int32, (2 * blk, 2 * blk), 1)
    later_scr[...] = ((r // blk == c // blk) & (r % blk > c % blk)).astype(BF16)
    causal_scr[...] = (lax.broadcasted_iota(jnp.int32, (tq, 2 * blk), 1) % blk
                       < lax.broadcasted_iota(jnp.int32, (tq, 2 * blk), 0)).astype(F32)

    def scores(q_rows, n_rows, kb, mask):
        qt = q_ref[pl.ds(q_rows, n_rows), :] * jnp.asarray(HEAD_DIM ** -0.5, BF16)
        z = lax.dot_general(qt, k_scr[kb], NT_DIMS, preferred_element_type=F32)
        log1p_e = jnp.log(1.0 + jnp.exp(_neg_abs(z)))
        log_beta = jnp.minimum(z, 0.0) - log1p_e
        log_fail = log_beta - z
        if mask is not None:
            log_fail = log_fail * mask
        after = jnp.dot(log_fail.astype(BF16), later_scr[...], preferred_element_type=F32)
        totals = [after[:, h * blk:h * blk + 1] + log_fail[:, h * blk:h * blk + 1]
                  for h in range(2)]
        return log_beta + after, totals

    def weights(parts, carry, mask):
        log_w, totals = parts
        a = jnp.exp(log_w + carry)
        if mask is not None:
            a = a * mask
        halves = [carry[:, h * blk:(h + 1) * blk] + totals[h] for h in range(2)]
        return a.astype(BF16), jnp.concatenate(halves, axis=1)

    def tile(qi, carry_):
        row0 = pl.multiple_of(qi * tq, tq)
        first = qi * per_tile
        order = list(reversed(range(per_tile)))
        masks = {j: causal_scr[0:tq - j * blk, :] for j in order}
        parts = {j: scores(row0 + j * blk, tq - j * blk, first + j, masks[j]) for j in order}
        carry = jnp.zeros((tq, 2 * blk), F32)
        pv = jnp.zeros((tq, LANES), F32)
        for j in order:
            r0 = j * blk
            a, tail = weights(parts[j], carry[r0:], masks[j])
            upd = pv[r0:] + jnp.dot(a, v_scr[first + j], preferred_element_type=F32)
            carry = tail if r0 == 0 else jnp.concatenate([carry[:r0], tail], axis=0)
            pv = upd if r0 == 0 else jnp.concatenate([pv[:r0], upd], axis=0)
        acc_ref[...] = pv
        carry_ref[...] = carry

        def below(state):
            i, _ = state
            kbs = [first - 1 - (SB_GROUP * i + u) for u in range(SB_GROUP)]
            parts = [scores(row0, tq, kb, None) for kb in kbs]
            carry = carry_ref[...]
            pv = jnp.zeros((tq, LANES), F32)
            for kb, p in zip(kbs, parts):
                a, carry = weights(p, carry, None)
                pv = pv + jnp.dot(a, v_scr[kb], preferred_element_type=F32)
            acc_ref[...] += pv
            carry_ref[...] = carry
            return i + 1, jnp.max(carry)

        lax.while_loop(lambda s: (s[0] < qi * (per_tile // SB_GROUP)) & (s[1] > SB_EXP_ZERO), below,
                       (jnp.int32(0), jnp.float32(0.0)))
        o_ref[pl.ds(row0, tq), :] = acc_ref[...].astype(BF16)
        return carry_

    lax.fori_loop(0, seq // tq, tile, 0)


def _sb_attention(qkv, batch, seq):
    tq = min(seq, SB_Q_TILE)
    pairs = SB_W // LANES
    q0 = (SWA_Q_W + 2 * SWA_KV_W) // LANES
    k0, v0 = q0 + pairs, q0 + 2 * pairs
    return pl.pallas_call(
        functools.partial(_sb_kernel, seq),
        grid=(batch, pairs),
        in_specs=[
            pl.BlockSpec((seq, LANES), lambda b, p: (b, q0 + p)),
            pl.BlockSpec((seq, LANES), lambda b, p: (b, k0 + p)),
            pl.BlockSpec((seq, LANES), lambda b, p: (b, v0 + p)),
        ],
        out_specs=pl.BlockSpec((seq, LANES), lambda b, p: (b, p)),
        out_shape=jax.ShapeDtypeStruct((batch * seq, SB_W), BF16),
        scratch_shapes=[
            pltpu.VMEM((seq // SB_BLOCK, 2 * SB_BLOCK, LANES), BF16),
            pltpu.VMEM((seq // SB_BLOCK, 2 * SB_BLOCK, LANES), BF16),
            pltpu.VMEM((2 * SB_BLOCK, 2 * SB_BLOCK), BF16),
            pltpu.VMEM((tq, 2 * SB_BLOCK), F32),
            pltpu.VMEM((tq, LANES), F32),
            pltpu.VMEM((tq, 2 * SB_BLOCK), F32),
        ],
        compiler_params=_params("arbitrary", "arbitrary"),
    )(qkv, qkv, qkv)


def _hgrn_kernel(seq, q_ref, f_ref, i_ref, g_ref, lb_ref, on_ref, o_ref,
                 qs_scr, ks_scr, b_scr, o_scr, dec_scr, st_scr, qrow_scr):
    ch = HGRN_CHUNK
    sc = HGRN_STATE_CHUNK
    hd = HGRN_HEAD_DIM
    heads = HGRN_HEADS_PER_STEP
    width = heads * hd
    pt = min(seq, HGRN_PRE_TILE)

    r = lax.broadcasted_iota(jnp.int32, (pt, pt), 0)
    c = lax.broadcasted_iota(jnp.int32, (pt, pt), 1)
    same_sub = (r // ch) == (c // ch)
    same_chunk = (r // sc) == (c // sc)
    m_diag = same_sub & (c <= r)
    m_prev = same_chunk & ((r // ch) == (c // ch) + 1)
    twice = lambda m: jnp.concatenate([m.astype(BF16)] * 2, axis=1)
    cum_sub, tot_sub, tot_chunk = twice(m_diag), twice(same_sub), twice(same_chunk)
    first_sub = (lax.broadcasted_iota(jnp.int32, (pt, 1), 0) // ch) % (sc // ch) == 0
    s_idx = lax.broadcasted_iota(jnp.int32, (ch, 1), 0)
    row8 = lax.broadcasted_iota(jnp.int32, (8, 1), 0)
    lb = lb_ref[...]

    def exact_diagonal(row0):
        def sub_chunk(n, carry):
            r0 = pl.multiple_of(row0 + n * ch, ch)
            for h in range(heads):
                cols = slice(h * hd, (h + 1) * hd)
                b_c = b_scr[pl.ds(r0, ch), cols]
                _, sig_neg = _sigmoid_pair(f_ref[pl.ds(r0, ch), cols].astype(F32))
                k_c = (1.0 - lb[:, cols]) * sig_neg
                v_c = i_ref[pl.ds(r0, ch), cols].astype(F32)
                qrow_scr[...] = q_ref[pl.ds(r0, ch), cols].astype(F32)

                def row(t, cc, b_c=b_c, k_c=k_c, v_c=v_c, cols=cols):
                    g0 = pl.multiple_of((t // 8) * 8, 8)
                    pick = row8 == (t % 8)
                    take = lambda blk: jnp.sum(jnp.where(pick, blk, 0.0), axis=0, keepdims=True)
                    b_t = take(b_scr[pl.ds(r0 + g0, 8), cols])
                    q_t = take(qrow_scr[pl.ds(g0, 8), :])
                    w = jnp.where(s_idx <= t, jnp.exp(jnp.minimum(b_t - b_c, 0.0)), 0.0)
                    sc_t = jnp.sum(q_t * k_c * w, axis=-1, keepdims=True)
                    o_t = jnp.sum(sc_t * v_c, axis=0, keepdims=True)
                    o_scr[pl.ds(r0 + g0, 8), cols] += jnp.where(pick, o_t, 0.0)
                    return cc

                lax.fori_loop(0, ch, row, 0)
            return carry

        lax.fori_loop(0, pt // ch, sub_chunk, 0)

    def gates(t):
        rows = slice(t * pt, (t + 1) * pt)
        sig, sig_neg = _sigmoid_pair(f_ref[rows, :].astype(F32))
        hi, lo = _split_bf16(jnp.log(lb + (1.0 - lb) * sig))
        return (1.0 - lb) * sig_neg, jnp.concatenate([hi, lo], axis=0)

    def decays(parts):
        return [jnp.dot(m, parts, preferred_element_type=F32)
                for m in (cum_sub, tot_sub, tot_chunk)]

    def factors(t, kk, b, end_sub, end_chunk):
        rows = slice(t * pt, (t + 1) * pt)
        before = jnp.where(first_sub, 0.0, end_chunk - end_sub)
        qd = q_ref[rows, :].astype(F32) * jnp.exp(b)
        qs_scr[rows, :] = (qd * jnp.exp(before)).astype(BF16)
        ks_scr[rows, :] = (kk * jnp.exp(end_chunk - b - before)).astype(BF16)
        b_scr[rows, :] = b
        for i in range(pt // sc):
            dec_scr[t * (pt // sc) + i] = jnp.broadcast_to(
                jnp.exp(end_chunk[i * sc:i * sc + 1, :]), (8, width))
        safe = jnp.min(end_sub) > HGRN_SAFE_LOG_DECAY
        return (qd.astype(BF16), (kk * jnp.exp(-b)).astype(BF16),
                (kk * jnp.exp(end_sub - b)).astype(BF16), safe)

    def intra(t, qd_b, kd_b, ke_b, safe):
        rows = slice(t * pt, (t + 1) * pt)
        v = i_ref[rows, :]
        for h in range(heads):
            cols = slice(h * hd, (h + 1) * hd)
            a_diag = lax.dot_general(qd_b[:, cols], kd_b[:, cols], NT_DIMS,
                                     preferred_element_type=F32)
            a_prev = lax.dot_general(qd_b[:, cols], ke_b[:, cols], NT_DIMS,
                                     preferred_element_type=F32)
            a = jnp.where(m_diag, jnp.where(safe, a_diag, 0.0), jnp.where(m_prev, a_prev, 0.0))
            o_scr[rows, cols] = jnp.dot(a.astype(BF16), v[:, cols], preferred_element_type=F32)

    n_tiles = seq // pt
    g_out, d_out, f_out = {}, {}, {}
    for step in range(n_tiles + 3):
        if step < n_tiles:
            g_out[step] = gates(step)
        if 0 <= step - 1 < n_tiles:
            d_out[step - 1] = decays(g_out[step - 1][1])
        if 0 <= step - 2 < n_tiles:
            f_out[step - 2] = factors(step - 2, g_out[step - 2][0], *d_out[step - 2])
        if 0 <= step - 3 < n_tiles:
            intra(step - 3, *f_out[step - 3])

    for t in range(n_tiles):
        @pl.when(jnp.logical_not(f_out[t][3]))
        def _(t=t):
            exact_diagonal(t * pt)

    st_scr[...] = jnp.zeros_like(st_scr)

    def state_step(n, carry):
        r0 = pl.multiple_of(n * sc, sc)
        decay = dec_scr[n]
        for h in range(heads):
            cols = slice(h * hd, (h + 1) * hd)
            st = st_scr[h]
            o_scr[pl.ds(r0, sc), cols] += lax.dot_general(
                qs_scr[pl.ds(r0, sc), cols], st.astype(BF16), NT_DIMS, preferred_element_type=F32)
            kv = lax.dot_general(i_ref[pl.ds(r0, sc), cols], ks_scr[pl.ds(r0, sc), cols], TN_DIMS,
                                 preferred_element_type=F32)
            st_scr[h] = st * decay[0:1, cols] + kv
        return carry

    lax.fori_loop(0, seq // sc, state_step, 0, unroll=4)

    for t in range(seq // pt):
        rows = slice(t * pt, (t + 1) * pt)
        gate = g_ref[rows, :].astype(F32)
        silu = gate * (1.0 / (1.0 + jnp.exp(-gate)))
        for h in range(heads):
            cols = slice(h * hd, (h + 1) * hd)
            y = _rms(o_scr[rows, cols], on_ref[...])
            o_ref[rows, cols] = (y * silu[:, cols]).astype(BF16)


def _hgrn_mixer_core(proj, lower_bound, o_gain, batch, seq):
    heads = HGRN_HEADS_PER_STEP
    width = heads * HGRN_HEAD_DIM
    groups = HGRN_HEADS // heads
    assert seq % HGRN_PRE_TILE == 0 and HGRN_PRE_TILE % HGRN_STATE_CHUNK == 0
    section = lambda k: pl.BlockSpec((seq, width), lambda b, g, k=k: (b, k * groups + g))
    return pl.pallas_call(
        functools.partial(_hgrn_kernel, seq),
        grid=(batch, groups),
        in_specs=[section(0), section(1), section(2), section(3),
                  pl.BlockSpec((1, width), lambda b, g: (0, g)),
                  pl.BlockSpec((1, HGRN_HEAD_DIM), lambda b, g: (0, 0))],
        out_specs=pl.BlockSpec((seq, width), lambda b, g: (b, g)),
        out_shape=jax.ShapeDtypeStruct((batch * seq, D_MODEL), BF16),
        scratch_shapes=[
            pltpu.VMEM((seq, width), BF16),
            pltpu.VMEM((seq, width), BF16),
            pltpu.VMEM((seq, width), F32),
            pltpu.VMEM((seq, width), F32),
            pltpu.VMEM((seq // HGRN_STATE_CHUNK, 8, width), F32),
            pltpu.VMEM((heads, HGRN_HEAD_DIM, HGRN_HEAD_DIM), F32),
            pltpu.VMEM((HGRN_CHUNK, HGRN_HEAD_DIM), F32),
        ],
        compiler_params=_params("arbitrary", "arbitrary"),
    )(proj, proj, proj, proj, lower_bound.reshape(1, D_MODEL), o_gain.reshape(1, HGRN_HEAD_DIM))


def _xa_kv_kernel(mem_ref, g_ref, w_ref, kn_ref, k_ref, v_ref):
    mem_n = _rms(mem_ref[...], g_ref[...]).astype(BF16)
    kv = jnp.dot(mem_n, w_ref[...], preferred_element_type=F32)
    for h in range(XA_HEADS):
        cols = slice(h * XA_HEAD_DIM, (h + 1) * XA_HEAD_DIM)
        k_ref[:, cols] = _rms(kv[:, cols], kn_ref[...]).astype(BF16)
    v_ref[...] = kv[:, D_MODEL:].astype(BF16)


def _xa_kv(mem, gain, w_kv, k_gain):
    m, d = mem.shape
    tm = min(m, TOKEN_TILE)
    row = lambda i: (i, 0)
    return pl.pallas_call(
        _xa_kv_kernel,
        grid=(m // tm,),
        in_specs=[pl.BlockSpec((tm, d), row), pl.BlockSpec((1, d), lambda i: (0, 0)),
                  _resident(w_kv.shape), pl.BlockSpec((1, XA_HEAD_DIM), lambda i: (0, 0))],
        out_specs=[pl.BlockSpec((tm, d), row), pl.BlockSpec((tm, d), row)],
        out_shape=[jax.ShapeDtypeStruct((m, d), BF16)] * 2,
        compiler_params=_params("arbitrary"),
    )(mem, gain.reshape(1, d), w_kv, k_gain.reshape(1, XA_HEAD_DIM))


def _xa_kernel(n_in, x_ref, *refs):
    a_refs, w_refs = refs[:n_in], refs[n_in:2 * n_in]
    g_ref, wq_ref, qn_ref, k_ref, v_ref, wo_ref, o_ref = refs[2 * n_in:]
    x = x_ref[...]
    for a_ref, w_ref in zip(a_refs, w_refs):
        x = x + jnp.dot(a_ref[...], w_ref[...], preferred_element_type=F32)
    h = _rms(x, g_ref[...]).astype(BF16)
    q = jnp.dot(h, wq_ref[...], preferred_element_type=F32)
    heads = []
    for hd in range(XA_HEADS):
        cols = slice(hd * XA_HEAD_DIM, (hd + 1) * XA_HEAD_DIM)
        qh = (_rms(q[:, cols], qn_ref[...]) * XA_HEAD_DIM ** -0.5).astype(BF16)
        s = lax.dot_general(qh, k_ref[:, cols], NT_DIMS, preferred_element_type=F32)
        p = jnp.exp(s - jnp.max(s, axis=-1, keepdims=True))
        p = p * (1.0 / jnp.sum(p, axis=-1, keepdims=True))
        heads.append(jnp.dot(p.astype(BF16), v_ref[:, cols],
                             preferred_element_type=F32).astype(BF16))
    o = jnp.concatenate(heads, axis=1)
    o_ref[...] = x + jnp.dot(o, wo_ref[...], preferred_element_type=F32)


def _cross_attention(x, acts, weights, gain, w_q, q_gain, k, v, w_o, seq, n_mem):
    m, d = x.shape
    tm = min(seq, TOKEN_TILE)
    per_seq = seq // tm
    row = lambda i: (i, 0)
    mem_row = lambda i: (i // per_seq, 0)
    return pl.pallas_call(
        functools.partial(_xa_kernel, len(acts)),
        grid=(m // tm,),
        in_specs=[pl.BlockSpec((tm, d), row)]
        + [pl.BlockSpec((tm, a.shape[1]), row) for a in acts]
        + [_resident(w.shape) for w in weights]
        + [pl.BlockSpec((1, d), lambda i: (0, 0)),
           _resident(w_q.shape), pl.BlockSpec((1, XA_HEAD_DIM), lambda i: (0, 0)),
           pl.BlockSpec((n_mem, d), mem_row), pl.BlockSpec((n_mem, d), mem_row),
           _resident(w_o.shape)],
        out_specs=pl.BlockSpec((tm, d), row),
        out_shape=jax.ShapeDtypeStruct((m, d), F32),
        compiler_params=_params("arbitrary"),
    )(x, *acts, *weights, gain.reshape(1, d), w_q, q_gain.reshape(1, XA_HEAD_DIM), k, v, w_o)


def _ffn_kernel(per_seq, d_ff, x_ref, g_ref, wup_ref, cw_ref, cb_ref, wdn_ref, o_ref,
                halo_ref):
    tm = x_ref.shape[0]
    fc = FF_CHUNK
    n_chunks = d_ff // fc
    groups = tm // 8

    @pl.when(pl.program_id(0) % per_seq == 0)
    def _():
        halo_ref[...] = jnp.zeros_like(halo_ref)

    x = x_ref[...]
    h = _rms(x, g_ref[...]).astype(BF16)
    sub = lax.broadcasted_iota(jnp.int32, (groups, 8, fc), 1)


    def up_proj(j):
        return [jnp.dot(h, wup_ref[:, c0:c0 + fc], preferred_element_type=F32)
                for c0 in (j * fc, d_ff + j * fc)]

    def conv(u, col0, slot):
        cols = slice(col0, col0 + fc)
        prev = halo_ref[slot]
        halo_ref[slot] = u[tm - 8:, :]
        grouped = jnp.concatenate([prev, u], axis=0).reshape(groups + 1, 8, fc)
        out = cw_ref[2:3, cols] * u + cb_ref[:, cols]
        for n in (1, 2):
            rot = pltpu.roll(grouped, n, 1)
            shifted = jnp.where(sub < n, rot[:groups], rot[1:]).reshape(tm, fc)
            out = out + cw_ref[2 - n:3 - n, cols] * shifted
        return out

    nxt = up_proj(0)
    acts = []
    for j in range(n_chunks):
        cur = nxt
        if j + 1 < n_chunks:
            nxt = up_proj(j + 1)
        gate = conv(cur[0], j * fc, j)
        up = conv(cur[1], d_ff + j * fc, n_chunks + j)
        acts.append((gate * (1.0 / (1.0 + jnp.exp(-gate))) * up).astype(BF16))
    o_ref[...] = x + jnp.dot(jnp.concatenate(acts, axis=1), wdn_ref[...],
                             preferred_element_type=F32)


def _conv_ffn(x, gain, w_up, conv_w, conv_b, w_down, seq):
    m, d = x.shape
    d_ff = w_down.shape[0]
    assert d_ff % FF_CHUNK == 0
    tm = min(seq, TOKEN_TILE)
    row = lambda i: (i, 0)
    const = lambda i: (0, 0)
    return pl.pallas_call(
        functools.partial(_ffn_kernel, seq // tm, d_ff),
        grid=(m // tm,),
        in_specs=[pl.BlockSpec((tm, d), row), pl.BlockSpec((1, d), const),
                  _resident(w_up.shape), pl.BlockSpec(conv_w.shape, const),
                  pl.BlockSpec((1, 2 * d_ff), const), _resident(w_down.shape)],
        out_specs=pl.BlockSpec((tm, d), row),
        out_shape=jax.ShapeDtypeStruct((m, d), F32),
        scratch_shapes=[pltpu.VMEM((2 * d_ff // FF_CHUNK, 8, FF_CHUNK), F32)],
        compiler_params=_params("arbitrary"),
    )(x, gain.reshape(1, d), w_up, conv_w, conv_b.reshape(1, 2 * d_ff), w_down)


def kernel(x, mem, positions, norm_mix, norm_cross, norm_mem, norm_ffn, ab_w_in, ab_w_out, swa_q_norm, swa_k_norm, swa_sinks, hgrn_w_in, hgrn_w_out, hgrn_o_norm, hgrn_lb, xa_w_q, xa_w_kv, xa_w_o, xa_q_norm, xa_k_norm, ffn_w_up, ffn_conv_w, ffn_conv_b, ffn_w_down):
    batch, seq, d = x.shape
    n_mem = mem.shape[1]
    depth = norm_mix.shape[0]
    assert d == D_MODEL and seq % TOKEN_TILE == 0
    assert (batch * n_mem) % min(batch * n_mem, TOKEN_TILE) == 0
    bf = lambda w: w.astype(BF16)

    p_lb = jax.nn.softmax(hgrn_lb.astype(F32), axis=0)
    lower_bounds = jnp.cumsum(p_lb, axis=0) - p_lb[0]

    xf = x.reshape(batch * seq, d)
    memf = mem.reshape(batch * n_mem, d)
    cos, sin = _rope_tables(positions)

    for l in range(depth):
        if l % 2 == 0:
            e = l // 2
            qkv = _norm_matmul(xf, norm_mix[l], bf(ab_w_in[e]), tn=AB_IN // 3)
            out_a = _swa_attention(qkv, cos, sin, swa_q_norm[e], swa_k_norm[e], swa_sinks[e],
                                   batch, seq)
            out_b = _sb_attention(qkv, batch, seq)
            w_out = bf(ab_w_out[e])
            mixed, mix_w = [out_a, out_b], [w_out[:SWA_Q_W], w_out[SWA_Q_W:]]
        else:
            o = l // 2
            proj = _norm_matmul(xf, norm_mix[l], bf(hgrn_w_in[o]), tn=D_MODEL)
            mixed = [_hgrn_mixer_core(proj, lower_bounds[l], hgrn_o_norm[o], batch, seq)]
            mix_w = [bf(hgrn_w_out[o])]
        k, v = _xa_kv(memf, norm_mem[l], bf(xa_w_kv[l]), xa_k_norm[l])
        xf = _cross_attention(xf, mixed, mix_w, norm_cross[l], bf(xa_w_q[l]), xa_q_norm[l], k, v,
                              bf(xa_w_o[l]), seq, n_mem)
        xf = _conv_ffn(xf, norm_ffn[l], bf(ffn_w_up[l]), ffn_conv_w[l], ffn_conv_b[l],
                       bf(ffn_w_down[l]), seq)
    return xf.reshape(batch, seq, d)
```

```python
import functools

import jax
import jax.numpy as jnp
from jax import lax
from jax.experimental import pallas as pl
from jax.experimental.pallas import tpu as pltpu

F32 = jnp.float32
BF16 = jnp.bfloat16

D_MODEL = 1024
HEAD_DIM = 64
SWA_HEADS = 8
SWA_KV_HEADS = 2
WINDOW = 128
SB_HEADS = 8
SB_BLOCK = 128
HGRN_HEAD_DIM = 128
HGRN_HEADS = D_MODEL // HGRN_HEAD_DIM
XA_HEADS = 4
XA_HEAD_DIM = D_MODEL // XA_HEADS
ROPE_THETA = 10000.0
EPS = 1e-6
SWA_Q_W = SWA_HEADS * HEAD_DIM
SWA_KV_W = SWA_KV_HEADS * HEAD_DIM
SB_W = SB_HEADS * HEAD_DIM
AB_IN = SWA_Q_W + 2 * SWA_KV_W + 3 * SB_W

LANES = 128
MXU_DIM = 256
TOKEN_TILE = 512
FF_CHUNK = 256
HGRN_CHUNK = 32
HGRN_STATE_CHUNK = 64
HGRN_PRE_TILE = 256
HGRN_HEADS_PER_STEP = 4
HGRN_SAFE_LOG_DECAY = -60.0
SB_Q_TILE = 512
SB_GROUP = 2
SB_EXP_ZERO = -105.0
VMEM_LIMIT = 56 * 1024 * 1024

NT_DIMS = (((1,), (1,)), ((), ()))
TN_DIMS = (((0,), (0,)), ((), ()))


def _params(*sem):
    return pltpu.CompilerParams(dimension_semantics=sem, vmem_limit_bytes=VMEM_LIMIT)


def _resident(shape):
    zeros = (0,) * len(shape)
    return pl.BlockSpec(shape, lambda *_: zeros, pipeline_mode=pl.Buffered(1))


def _rms(x, gain):
    ms = jnp.mean(x * x, axis=-1, keepdims=True)
    return x * lax.rsqrt(ms + EPS) * gain


def _split_bf16(x):
    hi = x.astype(BF16)
    lo = (x - hi.astype(F32)).astype(BF16)
    return hi, lo


def _sigmoid_pair(x):
    e = jnp.exp(-jnp.abs(x))
    big = 1.0 / (1.0 + e)
    small = e * big
    pos = x >= 0
    return jnp.where(pos, big, small), jnp.where(pos, small, big)


def _norm_matmul_kernel(tn, x_ref, g_ref, w_ref, o_ref):
    h = _rms(x_ref[...], g_ref[...]).astype(BF16)
    for j in range(w_ref.shape[1] // tn):
        cols = slice(j * tn, (j + 1) * tn)
        o_ref[:, cols] = jnp.dot(h, w_ref[:, cols], preferred_element_type=F32).astype(BF16)


def _norm_matmul(x, gain, w, tn):
    m, d = x.shape
    n = w.shape[1]
    assert n % tn == 0
    tm = TOKEN_TILE
    row = lambda i: (i, 0)
    return pl.pallas_call(
        functools.partial(_norm_matmul_kernel, tn),
        grid=(m // tm,),
        in_specs=[pl.BlockSpec((tm, d), row), pl.BlockSpec((1, d), lambda i: (0, 0)),
                  _resident(w.shape)],
        out_specs=pl.BlockSpec((tm, n), row),
        out_shape=jax.ShapeDtypeStruct((m, n), BF16),
        compiler_params=_params("arbitrary"),
    )(x, gain.reshape(1, d), w)


def _hgrn_proj_kernel(x_ref, g_ref, w_ref, lb_ref, o_ref):
    d = x_ref.shape[1]
    h = _rms(x_ref[...], g_ref[...]).astype(BF16)
    tn = MXU_DIM
    for c in range(d // tn):
        cols = slice(c * tn, (c + 1) * tn)
        proj = lambda k: jnp.dot(h, w_ref[:, k * d + c * tn:k * d + (c + 1) * tn],
                                 preferred_element_type=F32)
        sec = lambda k: slice(k * d + c * tn, k * d + (c + 1) * tn)
        lb = lb_ref[:, cols]
        sig, sig_neg = _sigmoid_pair(proj(1))
        o_ref[:, sec(0)] = proj(0).astype(BF16)
        hi, lo = _split_bf16(jnp.log(lb + (1.0 - lb) * sig))
        o_ref[:, sec(1)] = hi
        o_ref[:, sec(4)] = lo
        o_ref[:, sec(5)] = ((1.0 - lb) * sig_neg).astype(BF16)
        gate = proj(3)
        o_ref[:, sec(2)] = proj(2).astype(BF16)
        o_ref[:, sec(3)] = (gate * (1.0 / (1.0 + jnp.exp(-gate)))).astype(BF16)


def _hgrn_proj(x, gain, w, lower_bound):
    m, d = x.shape
    tm = TOKEN_TILE
    row = lambda i: (i, 0)
    const = lambda i: (0, 0)
    return pl.pallas_call(
        _hgrn_proj_kernel,
        grid=(m // tm,),
        in_specs=[pl.BlockSpec((tm, d), row), pl.BlockSpec((1, d), const), _resident(w.shape),
                  pl.BlockSpec((1, d), const)],
        out_specs=pl.BlockSpec((tm, 6 * d), row),
        out_shape=jax.ShapeDtypeStruct((m, 6 * d), BF16),
        compiler_params=_params("arbitrary"),
    )(x, gain.reshape(1, d), w, lower_bound.reshape(1, d))


def _rope_table_kernel(pos_ref, inv_ref, cos_ref, sin_ref):
    ang = pos_ref[...] * inv_ref[...]
    lane = lax.broadcasted_iota(jnp.int32, ang.shape, 1)
    first_half = (lane % HEAD_DIM) < HEAD_DIM // 2
    cos_ref[...] = jnp.cos(ang)
    sin = jnp.sin(ang)
    sin_ref[...] = jnp.where(first_half, -sin, sin)


def _rope_tables(positions):
    m = positions.size
    tm = TOKEN_TILE
    half = HEAD_DIM // 2
    inv_freq = ROPE_THETA ** (-jnp.arange(0, HEAD_DIM, 2, dtype=F32) / HEAD_DIM)
    inv = jnp.tile(inv_freq, LANES // half).reshape(1, LANES)
    pos = positions.astype(F32).reshape(m, 1)
    row = lambda i: (i, 0)
    return pl.pallas_call(
        _rope_table_kernel,
        grid=(m // tm,),
        in_specs=[pl.BlockSpec((tm, 1), row), pl.BlockSpec((1, LANES), lambda i: (0, 0))],
        out_specs=[pl.BlockSpec((tm, LANES), row), pl.BlockSpec((tm, LANES), row)],
        out_shape=[jax.ShapeDtypeStruct((m, LANES), F32)] * 2,
        compiler_params=_params("arbitrary"),
    )(pos, inv)


def _group_ones(width):
    r = lax.broadcasted_iota(jnp.int32, (width, width), 0) // HEAD_DIM
    c = lax.broadcasted_iota(jnp.int32, (width, width), 1) // HEAD_DIM
    return (r == c).astype(BF16)


def _head_norm_rope(x, gain, cos, sin):
    hi, lo = _split_bf16(x * x)
    ones = _group_ones(LANES)
    ss = (jnp.dot(hi, ones, preferred_element_type=F32)
          + jnp.dot(lo, ones, preferred_element_type=F32))
    y = x * lax.rsqrt(ss * (1.0 / HEAD_DIM) + EPS) * gain
    lane = lax.broadcasted_iota(jnp.int32, y.shape, 1)
    first_half = (lane % HEAD_DIM) < HEAD_DIM // 2
    partner = jnp.where(first_half,
                        pltpu.roll(y, LANES - HEAD_DIM // 2, 1),
                        pltpu.roll(y, HEAD_DIM // 2, 1))
    return y * cos + partner * sin


def _swa_kernel(seq, qkv_ref, cos_ref, sin_ref, qg_ref, kg_ref, sink_ref, o_ref,
                q_scr, k_scr, v_scr):
    rt = min(seq, TOKEN_TILE)
    n_q_chunks = SWA_Q_W // LANES
    for r in range(seq // rt):
        rows = slice(r * rt, (r + 1) * rt)
        cos, sin = cos_ref[rows, :], sin_ref[rows, :]
        for c in range(n_q_chunks):
            cols = slice(c * LANES, (c + 1) * LANES)
            xq = qkv_ref[rows, cols].astype(F32)
            q = _head_norm_rope(xq, qg_ref[...], cos, sin)
            q_scr[c, rows, :] = (q * HEAD_DIM ** -0.5).astype(BF16)
        xk = qkv_ref[rows, SWA_Q_W:SWA_Q_W + SWA_KV_W].astype(F32)
        k = _head_norm_rope(xk, kg_ref[...], cos, sin)
        v = qkv_ref[rows, SWA_Q_W + SWA_KV_W:SWA_Q_W + 2 * SWA_KV_W].astype(F32)
        low = lax.broadcasted_iota(jnp.int32, k.shape, 1) < HEAD_DIM
        for t, scr in ((k, k_scr), (v, v_scr)):
            swapped = pltpu.roll(t, HEAD_DIM, 1)
            scr[0, 0, rows, :] = jnp.where(low, t, 0.0).astype(BF16)
            scr[1, 0, rows, :] = jnp.where(low, 0.0, swapped).astype(BF16)
            scr[0, 1, rows, :] = jnp.where(low, swapped, 0.0).astype(BF16)
            scr[1, 1, rows, :] = jnp.where(low, 0.0, t).astype(BF16)

    w = WINDOW
    qi = lax.broadcasted_iota(jnp.int32, (2 * w, w), 0) % w
    kj = lax.broadcasted_iota(jnp.int32, (2 * w, w), 1)
    top = lax.broadcasted_iota(jnp.int32, (2 * w, 1), 0) < w
    cur_mask = kj <= qi

    def block(n, carry):
        r0 = pl.multiple_of(n * w, w)
        p0 = pl.multiple_of(jnp.maximum(n - 1, 0) * w, w)
        prev_mask = (kj > qi) & (n > 0)
        for g in range(SWA_KV_HEADS):
            qa = jnp.concatenate([q_scr[2 * g, pl.ds(r0, w), :],
                                  q_scr[2 * g + 1, pl.ds(r0, w), :]], axis=0)
            out = jnp.zeros((2 * w, LANES), F32)
            for slot in range(2):
                s_cur = lax.dot_general(qa, k_scr[slot, g, pl.ds(r0, w), :], NT_DIMS,
                                        preferred_element_type=F32)
                s_prev = lax.dot_general(qa, k_scr[slot, g, pl.ds(p0, w), :], NT_DIMS,
                                         preferred_element_type=F32)
                s_cur = jnp.where(cur_mask, s_cur, -jnp.inf)
                s_prev = jnp.where(prev_mask, s_prev, -jnp.inf)
                sink = jnp.where(top, sink_ref[4 * g + slot], sink_ref[4 * g + 2 + slot])
                m = jnp.maximum(jnp.max(jnp.maximum(s_cur, s_prev), axis=-1, keepdims=True), sink)
                p_cur = jnp.exp(s_cur - m)
                p_prev = jnp.exp(s_prev - m)
                den = jnp.sum(p_cur + p_prev, axis=-1, keepdims=True) + jnp.exp(sink - m)
                inv = 1.0 / den
                out = out + jnp.dot((p_cur * inv).astype(BF16), v_scr[slot, g, pl.ds(r0, w), :],
                                    preferred_element_type=F32)
                out = out + jnp.dot((p_prev * inv).astype(BF16), v_scr[slot, g, pl.ds(p0, w), :],
                                    preferred_element_type=F32)
            o_ref[pl.ds(r0, w), 2 * g * LANES:(2 * g + 1) * LANES] = out[:w].astype(BF16)
            o_ref[pl.ds(r0, w), (2 * g + 1) * LANES:(2 * g + 2) * LANES] = out[w:].astype(BF16)
        return carry

    lax.fori_loop(0, seq // w, block, 0, unroll=4)


def _swa_attention(qkv, cos, sin, q_gain, k_gain, sinks, batch, seq):
    width = SWA_Q_W + 2 * SWA_KV_W
    assert AB_IN % width == 0
    const = lambda b: (0, 0)
    return pl.pallas_call(
        functools.partial(_swa_kernel, seq),
        grid=(batch,),
        in_specs=[
            pl.BlockSpec((seq, width), lambda b: (b, 0)),
            pl.BlockSpec((seq, LANES), lambda b: (b, 0)),
            pl.BlockSpec((seq, LANES), lambda b: (b, 0)),
            pl.BlockSpec((1, LANES), const),
            pl.BlockSpec((1, LANES), const),
            pl.BlockSpec(memory_space=pltpu.SMEM),
        ],
        out_specs=pl.BlockSpec((seq, SWA_Q_W), lambda b: (b, 0)),
        out_shape=jax.ShapeDtypeStruct((batch * seq, SWA_Q_W), BF16),
        scratch_shapes=[
            pltpu.VMEM((SWA_Q_W // LANES, seq, LANES), BF16),
            pltpu.VMEM((2, SWA_KV_HEADS, seq, LANES), BF16),
            pltpu.VMEM((2, SWA_KV_HEADS, seq, LANES), BF16),
        ],
        compiler_params=_params("arbitrary"),
    )(qkv, cos, sin,
      jnp.tile(q_gain, LANES // HEAD_DIM).reshape(1, LANES),
      jnp.tile(k_gain, LANES // HEAD_DIM).reshape(1, LANES),
      sinks)


def _neg_abs(x):
    bits = lax.bitcast_convert_type(x, jnp.uint32) | jnp.uint32(0x80000000)
    return lax.bitcast_convert_type(bits, F32)


def _sb_kernel(seq, q_ref, k_ref, v_ref, o_ref, k_scr, v_scr, later_scr, causal_scr, acc_ref,
               carry_ref):
    blk = SB_BLOCK
    tq = min(seq, SB_Q_TILE)
    per_tile = tq // blk
    nb = seq // blk
    low = lax.broadcasted_iota(jnp.int32, (seq, LANES), 1) < HEAD_DIM
    for src, scr in ((k_ref, k_scr), (v_ref, v_scr)):
        full = src[...]
        zero = jnp.zeros_like(full)
        scr[:, 0:blk, :] = jnp.where(low, full, zero).reshape(nb, blk, LANES)
        scr[:, blk:2 * blk, :] = jnp.where(low, zero, full).reshape(nb, blk, LANES)

    r = lax.broadcasted_iota(jnp.int32, (2 * blk, 2 * blk), 0)
    c = lax.broadcasted_iota(jnp.int32, (2 * blk, 2 * blk), 1)
    later_scr[...] = ((r // blk == c // blk) & (r % blk > c % blk)).astype(BF16)
    causal_scr[...] = (lax.broadcasted_iota(jnp.int32, (tq, 2 * blk), 1) % blk
                       < lax.broadcasted_iota(jnp.int32, (tq, 2 * blk), 0)).astype(F32)

    def scores(q_rows, n_rows, kb, mask):
        qt = q_ref[pl.ds(q_rows, n_rows), :] * jnp.asarray(HEAD_DIM ** -0.5, BF16)
        z = lax.dot_general(qt, k_scr[kb], NT_DIMS, preferred_element_type=F32)
        log1p_e = jnp.log(1.0 + jnp.exp(_neg_abs(z)))
        log_beta = jnp.minimum(z, 0.0) - log1p_e
        log_fail = log_beta - z
        if mask is not None:
            log_fail = log_fail * mask
        after = jnp.dot(log_fail.astype(BF16), later_scr[...], preferred_element_type=F32)
        totals = [after[:, h * blk:h * blk + 1] + log_fail[:, h * blk:h * blk + 1]
                  for h in range(2)]
        return log_beta + after, totals

    def weights(parts, carry, mask):
        log_w, totals = parts
        a = jnp.exp(log_w + carry)
        if mask is not None:
            a = a * mask
        halves = [carry[:, h * blk:(h + 1) * blk] + totals[h] for h in range(2)]
        return a.astype(BF16), jnp.concatenate(halves, axis=1)

    def tile(qi, carry_):
        row0 = pl.multiple_of(qi * tq, tq)
        first = qi * per_tile
        order = list(reversed(range(per_tile)))
        masks = {j: causal_scr[0:tq - j * blk, :] for j in order}
        parts = {j: scores(row0 + j * blk, tq - j * blk, first + j, masks[j]) for j in order}
        carry = jnp.zeros((tq, 2 * blk), F32)
        pv = jnp.zeros((tq, LANES), F32)
        for j in order:
            r0 = j * blk
            a, tail = weights(parts[j], carry[r0:], masks[j])
            upd = pv[r0:] + jnp.dot(a, v_scr[first + j], preferred_element_type=F32)
            carry = tail if r0 == 0 else jnp.concatenate([carry[:r0], tail], axis=0)
            pv = upd if r0 == 0 else jnp.concatenate([pv[:r0], upd], axis=0)
        acc_ref[...] = pv
        carry_ref[...] = carry

        def below(state):
            i, _ = state
            kbs = [first - 1 - (SB_GROUP * i + u) for u in range(SB_GROUP)]
            parts = [scores(row0, tq, kb, None) for kb in kbs]
            carry = carry_ref[...]
            pv = jnp.zeros((tq, LANES), F32)
            for kb, p in zip(kbs, parts):
                a, carry = weights(p, carry, None)
                pv = pv + jnp.dot(a, v_scr[kb], preferred_element_type=F32)
            acc_ref[...] += pv
            carry_ref[...] = carry
            return i + 1, jnp.max(carry)

        lax.while_loop(lambda s: (s[0] < qi * (per_tile // SB_GROUP)) & (s[1] > SB_EXP_ZERO), below,
                       (jnp.int32(0), jnp.float32(0.0)))
        o_ref[pl.ds(row0, tq), :] = acc_ref[...].astype(BF16)
        return carry_

    lax.fori_loop(0, seq // tq, tile, 0)


def _sb_attention(qkv, batch, seq):
    tq = min(seq, SB_Q_TILE)
    pairs = SB_W // LANES
    q0 = (SWA_Q_W + 2 * SWA_KV_W) // LANES
    k0, v0 = q0 + pairs, q0 + 2 * pairs
    return pl.pallas_call(
        functools.partial(_sb_kernel, seq),
        grid=(batch, pairs),
        in_specs=[
            pl.BlockSpec((seq, LANES), lambda b, p: (b, q0 + p)),
            pl.BlockSpec((seq, LANES), lambda b, p: (b, k0 + p)),
            pl.BlockSpec((seq, LANES), lambda b, p: (b, v0 + p)),
        ],
        out_specs=pl.BlockSpec((seq, LANES), lambda b, p: (b, p)),
        out_shape=jax.ShapeDtypeStruct((batch * seq, SB_W), BF16),
        scratch_shapes=[
            pltpu.VMEM((seq // SB_BLOCK, 2 * SB_BLOCK, LANES), BF16),
            pltpu.VMEM((seq // SB_BLOCK, 2 * SB_BLOCK, LANES), BF16),
            pltpu.VMEM((2 * SB_BLOCK, 2 * SB_BLOCK), BF16),
            pltpu.VMEM((tq, 2 * SB_BLOCK), F32),
            pltpu.VMEM((tq, LANES), F32),
            pltpu.VMEM((tq, 2 * SB_BLOCK), F32),
        ],
        compiler_params=_params("arbitrary", "arbitrary"),
    )(qkv, qkv, qkv)


def _hgrn_kernel(seq, q_ref, hi_ref, i_ref, g_ref, lo_ref, k_ref, on_ref, o_ref,
                 qs_scr, ks_scr, b_scr, o_scr, dec_scr, st_scr, qrow_scr):
    ch = HGRN_CHUNK
    sc = HGRN_STATE_CHUNK
    hd = HGRN_HEAD_DIM
    heads = HGRN_HEADS_PER_STEP
    width = heads * hd
    pt = min(seq, HGRN_PRE_TILE)

    r = lax.broadcasted_iota(jnp.int32, (pt, pt), 0)
    c = lax.broadcasted_iota(jnp.int32, (pt, pt), 1)
    same_sub = (r // ch) == (c // ch)
    same_chunk = (r // sc) == (c // sc)
    m_diag = same_sub & (c <= r)
    m_prev = same_chunk & ((r // ch) == (c // ch) + 1)
    twice = lambda m: jnp.concatenate([m.astype(BF16)] * 2, axis=1)
    cum_sub, tot_sub, tot_chunk = twice(m_diag), twice(same_sub), twice(same_chunk)
    first_sub = (lax.broadcasted_iota(jnp.int32, (pt, 1), 0) // ch) % (sc // ch) == 0
    s_idx = lax.broadcasted_iota(jnp.int32, (ch, 1), 0)
    row8 = lax.broadcasted_iota(jnp.int32, (8, 1), 0)

    def exact_diagonal(row0):
        def sub_chunk(n, carry):
            r0 = pl.multiple_of(row0 + n * ch, ch)
            for h in range(heads):
                cols = slice(h * hd, (h + 1) * hd)
                b_c = b_scr[pl.ds(r0, ch), cols]
                k_c = k_ref[pl.ds(r0, ch), cols].astype(F32)
                v_c = i_ref[pl.ds(r0, ch), cols].astype(F32)
                qrow_scr[...] = q_ref[pl.ds(r0, ch), cols].astype(F32)

                def row(t, cc, b_c=b_c, k_c=k_c, v_c=v_c, cols=cols):
                    g0 = pl.multiple_of((t // 8) * 8, 8)
                    pick = row8 == (t % 8)
                    take = lambda blk: jnp.sum(jnp.where(pick, blk, 0.0), axis=0, keepdims=True)
                    b_t = take(b_scr[pl.ds(r0 + g0, 8), cols])
                    q_t = take(qrow_scr[pl.ds(g0, 8), :])
                    w = jnp.where(s_idx <= t, jnp.exp(jnp.minimum(b_t - b_c, 0.0)), 0.0)
                    sc_t = jnp.sum(q_t * k_c * w, axis=-1, keepdims=True)
                    o_t = jnp.sum(sc_t * v_c, axis=0, keepdims=True)
                    o_scr[pl.ds(r0 + g0, 8), cols] += jnp.where(pick, o_t, 0.0)
                    return cc

                lax.fori_loop(0, ch, row, 0)
            return carry

        lax.fori_loop(0, pt // ch, sub_chunk, 0)

    def gates(t):
        rows = slice(t * pt, (t + 1) * pt)
        return (k_ref[rows, :].astype(F32),
                jnp.concatenate([hi_ref[rows, :], lo_ref[rows, :]], axis=0))

    def decays(parts):
        return [jnp.dot(m, parts, preferred_element_type=F32)
                for m in (cum_sub, tot_sub, tot_chunk)]

    def factors(t, kk, b, end_sub, end_chunk):
        rows = slice(t * pt, (t + 1) * pt)
        before = jnp.where(first_sub, 0.0, end_chunk - end_sub)
        qd = q_ref[rows, :].astype(F32) * jnp.exp(b)
        qs_scr[rows, :] = (qd * jnp.exp(before)).astype(BF16)
        ks_scr[rows, :] = (kk * jnp.exp(end_chunk - b - before)).astype(BF16)
        b_scr[rows, :] = b
        for i in range(pt // sc):
            dec_scr[t * (pt // sc) + i] = jnp.broadcast_to(
                jnp.exp(end_chunk[i * sc:i * sc + 1, :]), (8, width))
        safe = jnp.min(end_sub) > HGRN_SAFE_LOG_DECAY
        return (qd.astype(BF16), (kk * jnp.exp(-b)).astype(BF16),
                (kk * jnp.exp(end_sub - b)).astype(BF16), safe)

    def intra(t, qd_b, kd_b, ke_b, safe):
        rows = slice(t * pt, (t + 1) * pt)
        v = i_ref[rows, :]
        for h in range(heads):
            cols = slice(h * hd, (h + 1) * hd)
            a_diag = lax.dot_general(qd_b[:, cols], kd_b[:, cols], NT_DIMS,
                                     preferred_element_type=F32)
            a_prev = lax.dot_general(qd_b[:, cols], ke_b[:, cols], NT_DIMS,
                                     preferred_element_type=F32)
            a = jnp.where(m_diag, jnp.where(safe, a_diag, 0.0), jnp.where(m_prev, a_prev, 0.0))
            o_scr[rows, cols] = jnp.dot(a.astype(BF16), v[:, cols], preferred_element_type=F32)

    n_tiles = seq // pt
    g_out, d_out, f_out = {}, {}, {}
    for step in range(n_tiles + 3):
        if step < n_tiles:
            g_out[step] = gates(step)
        if 0 <= step - 1 < n_tiles:
            d_out[step - 1] = decays(g_out[step - 1][1])
        if 0 <= step - 2 < n_tiles:
            f_out[step - 2] = factors(step - 2, g_out[step - 2][0], *d_out[step - 2])
        if 0 <= step - 3 < n_tiles:
            intra(step - 3, *f_out[step - 3])

    for t in range(n_tiles):
        @pl.when(jnp.logical_not(f_out[t][3]))
        def _(t=t):
            exact_diagonal(t * pt)

    st_scr[...] = jnp.zeros_like(st_scr)

    def state_step(n, carry):
        r0 = pl.multiple_of(n * sc, sc)
        decay = dec_scr[n]
        for h in range(heads):
            cols = slice(h * hd, (h + 1) * hd)
            st = st_scr[h]
            o_scr[pl.ds(r0, sc), cols] += lax.dot_general(
                qs_scr[pl.ds(r0, sc), cols], st.astype(BF16), NT_DIMS, preferred_element_type=F32)
            kv = lax.dot_general(i_ref[pl.ds(r0, sc), cols], ks_scr[pl.ds(r0, sc), cols], TN_DIMS,
                                 preferred_element_type=F32)
            st_scr[h] = st * decay[0:1, cols] + kv
        return carry

    lax.fori_loop(0, seq // sc, state_step, 0, unroll=4)

    for t in range(seq // pt):
        rows = slice(t * pt, (t + 1) * pt)
        silu = g_ref[rows, :].astype(F32)
        for h in range(heads):
            cols = slice(h * hd, (h + 1) * hd)
            y = _rms(o_scr[rows, cols], on_ref[...])
            o_ref[rows, cols] = (y * silu[:, cols]).astype(BF16)


def _hgrn_mixer_core(proj, o_gain, batch, seq):
    heads = HGRN_HEADS_PER_STEP
    width = heads * HGRN_HEAD_DIM
    groups = HGRN_HEADS // heads
    assert seq % HGRN_PRE_TILE == 0 and HGRN_PRE_TILE % HGRN_STATE_CHUNK == 0
    section = lambda k: pl.BlockSpec((seq, width), lambda b, g, k=k: (b, k * groups + g))
    return pl.pallas_call(
        functools.partial(_hgrn_kernel, seq),
        grid=(batch, groups),
        in_specs=[section(k) for k in range(6)]
        + [pl.BlockSpec((1, HGRN_HEAD_DIM), lambda b, g: (0, 0))],
        out_specs=pl.BlockSpec((seq, width), lambda b, g: (b, g)),
        out_shape=jax.ShapeDtypeStruct((batch * seq, D_MODEL), BF16),
        scratch_shapes=[
            pltpu.VMEM((seq, width), BF16),
            pltpu.VMEM((seq, width), BF16),
            pltpu.VMEM((seq, width), F32),
            pltpu.VMEM((seq, width), F32),
            pltpu.VMEM((seq // HGRN_STATE_CHUNK, 8, width), F32),
            pltpu.VMEM((heads, HGRN_HEAD_DIM, HGRN_HEAD_DIM), F32),
            pltpu.VMEM((HGRN_CHUNK, HGRN_HEAD_DIM), F32),
        ],
        compiler_params=_params("arbitrary", "arbitrary"),
    )(*([proj] * 6), o_gain.reshape(1, HGRN_HEAD_DIM))


def _xa_kv_kernel(mem_ref, g_ref, w_ref, kn_ref, k_ref, v_ref):
    mem_n = _rms(mem_ref[...], g_ref[...]).astype(BF16)
    kv = jnp.dot(mem_n, w_ref[...], preferred_element_type=F32)
    for h in range(XA_HEADS):
        cols = slice(h * XA_HEAD_DIM, (h + 1) * XA_HEAD_DIM)
        k_ref[:, cols] = _rms(kv[:, cols], kn_ref[...]).astype(BF16)
    v_ref[...] = kv[:, D_MODEL:].astype(BF16)


def _xa_kv(mem, gain, w_kv, k_gain):
    m, d = mem.shape
    tm = min(m, TOKEN_TILE)
    row = lambda i: (i, 0)
    return pl.pallas_call(
        _xa_kv_kernel,
        grid=(m // tm,),
        in_specs=[pl.BlockSpec((tm, d), row), pl.BlockSpec((1, d), lambda i: (0, 0)),
                  _resident(w_kv.shape), pl.BlockSpec((1, XA_HEAD_DIM), lambda i: (0, 0))],
        out_specs=[pl.BlockSpec((tm, d), row), pl.BlockSpec((tm, d), row)],
        out_shape=[jax.ShapeDtypeStruct((m, d), BF16)] * 2,
        compiler_params=_params("arbitrary"),
    )(mem, gain.reshape(1, d), w_kv, k_gain.reshape(1, XA_HEAD_DIM))


def _xa_kernel(n_in, x_ref, *refs):
    a_refs, w_refs = refs[:n_in], refs[n_in:2 * n_in]
    g_ref, wq_ref, qn_ref, k_ref, v_ref, wo_ref, o_ref = refs[2 * n_in:]
    x = x_ref[...]
    for a_ref, w_ref in zip(a_refs, w_refs):
        x = x + jnp.dot(a_ref[...], w_ref[...], preferred_element_type=F32)
    h = _rms(x, g_ref[...]).astype(BF16)
    q = jnp.dot(h, wq_ref[...], preferred_element_type=F32)
    heads = []
    for hd in range(XA_HEADS):
        cols = slice(hd * XA_HEAD_DIM, (hd + 1) * XA_HEAD_DIM)
        qh = (_rms(q[:, cols], qn_ref[...]) * XA_HEAD_DIM ** -0.5).astype(BF16)
        s = lax.dot_general(qh, k_ref[:, cols], NT_DIMS, preferred_element_type=F32)
        p = jnp.exp(s - jnp.max(s, axis=-1, keepdims=True))
        p = p * (1.0 / jnp.sum(p, axis=-1, keepdims=True))
        heads.append(jnp.dot(p.astype(BF16), v_ref[:, cols],
                             preferred_element_type=F32).astype(BF16))
    o = jnp.concatenate(heads, axis=1)
    o_ref[...] = x + jnp.dot(o, wo_ref[...], preferred_element_type=F32)


def _cross_attention(x, acts, weights, gain, w_q, q_gain, k, v, w_o, seq, n_mem):
    m, d = x.shape
    tm = min(seq, TOKEN_TILE)
    per_seq = seq // tm
    row = lambda i: (i, 0)
    mem_row = lambda i: (i // per_seq, 0)
    return pl.pallas_call(
        functools.partial(_xa_kernel, len(acts)),
        grid=(m // tm,),
        in_specs=[pl.BlockSpec((tm, d), row)]
        + [pl.BlockSpec((tm, a.shape[1]), row) for a in acts]
        + [_resident(w.shape) for w in weights]
        + [pl.BlockSpec((1, d), lambda i: (0, 0)),
           _resident(w_q.shape), pl.BlockSpec((1, XA_HEAD_DIM), lambda i: (0, 0)),
           pl.BlockSpec((n_mem, d), mem_row), pl.BlockSpec((n_mem, d), mem_row),
           _resident(w_o.shape)],
        out_specs=pl.BlockSpec((tm, d), row),
        out_shape=jax.ShapeDtypeStruct((m, d), F32),
        compiler_params=_params("arbitrary"),
    )(x, *acts, *weights, gain.reshape(1, d), w_q, q_gain.reshape(1, XA_HEAD_DIM), k, v, w_o)


def _ffn_kernel(per_seq, d_ff, x_ref, g_ref, wup_ref, cw_ref, cb_ref, wdn_ref, o_ref,
                halo_ref):
    tm = x_ref.shape[0]
    fc = FF_CHUNK
    n_chunks = d_ff // fc
    groups = tm // 8

    @pl.when(pl.program_id(0) % per_seq == 0)
    def _():
        halo_ref[...] = jnp.zeros_like(halo_ref)

    x = x_ref[...]
    h = _rms(x, g_ref[...]).astype(BF16)
    sub = lax.broadcasted_iota(jnp.int32, (groups, 8, fc), 1)


    def up_proj(j):
        return [jnp.dot(h, wup_ref[:, c0:c0 + fc], preferred_element_type=F32)
                for c0 in (j * fc, d_ff + j * fc)]

    def conv(u, col0, slot):
        cols = slice(col0, col0 + fc)
        prev = halo_ref[slot]
        halo_ref[slot] = u[tm - 8:, :]
        grouped = jnp.concatenate([prev, u], axis=0).reshape(groups + 1, 8, fc)
        out = cw_ref[2:3, cols] * u + cb_ref[:, cols]
        for n in (1, 2):
            rot = pltpu.roll(grouped, n, 1)
            shifted = jnp.where(sub < n, rot[:groups], rot[1:]).reshape(tm, fc)
            out = out + cw_ref[2 - n:3 - n, cols] * shifted
        return out

    nxt = up_proj(0)
    acts = []
    for j in range(n_chunks):
        cur = nxt
        if j + 1 < n_chunks:
            nxt = up_proj(j + 1)
        gate = conv(cur[0], j * fc, j)
        up = conv(cur[1], d_ff + j * fc, n_chunks + j)
        acts.append((gate * (1.0 / (1.0 + jnp.exp(-gate))) * up).astype(BF16))
    o_ref[...] = x + jnp.dot(jnp.concatenate(acts, axis=1), wdn_ref[...],
                             preferred_element_type=F32)


def _conv_ffn(x, gain, w_up, conv_w, conv_b, w_down, seq):
    m, d = x.shape
    d_ff = w_down.shape[0]
    assert d_ff % FF_CHUNK == 0
    tm = min(seq, TOKEN_TILE)
    row = lambda i: (i, 0)
    const = lambda i: (0, 0)
    return pl.pallas_call(
        functools.partial(_ffn_kernel, seq // tm, d_ff),
        grid=(m // tm,),
        in_specs=[pl.BlockSpec((tm, d), row), pl.BlockSpec((1, d), const),
                  _resident(w_up.shape), pl.BlockSpec(conv_w.shape, const),
                  pl.BlockSpec((1, 2 * d_ff), const), _resident(w_down.shape)],
        out_specs=pl.BlockSpec((tm, d), row),
        out_shape=jax.ShapeDtypeStruct((m, d), F32),
        scratch_shapes=[pltpu.VMEM((2 * d_ff // FF_CHUNK, 8, FF_CHUNK), F32)],
        compiler_params=_params("arbitrary"),
    )(x, gain.reshape(1, d), w_up, conv_w, conv_b.reshape(1, 2 * d_ff), w_down)


def kernel(x, mem, positions, norm_mix, norm_cross, norm_mem, norm_ffn, ab_w_in, ab_w_out, swa_q_norm, swa_k_norm, swa_sinks, hgrn_w_in, hgrn_w_out, hgrn_o_norm, hgrn_lb, xa_w_q, xa_w_kv, xa_w_o, xa_q_norm, xa_k_norm, ffn_w_up, ffn_conv_w, ffn_conv_b, ffn_w_down):
    batch, seq, d = x.shape
    n_mem = mem.shape[1]
    depth = norm_mix.shape[0]
    assert d == D_MODEL and seq % TOKEN_TILE == 0
    assert (batch * n_mem) % min(batch * n_mem, TOKEN_TILE) == 0
    bf = lambda w: w.astype(BF16)

    p_lb = jax.nn.softmax(hgrn_lb.astype(F32), axis=0)
    lower_bounds = jnp.cumsum(p_lb, axis=0) - p_lb[0]

    xf = x.reshape(batch * seq, d)
    memf = mem.reshape(batch * n_mem, d)
    cos, sin = _rope_tables(positions)

    for l in range(depth):
        if l % 2 == 0:
            e = l // 2
            qkv = _norm_matmul(xf, norm_mix[l], bf(ab_w_in[e]), tn=AB_IN // 3)
            out_a = _swa_attention(qkv, cos, sin, swa_q_norm[e], swa_k_norm[e], swa_sinks[e],
                                   batch, seq)
            out_b = _sb_attention(qkv, batch, seq)
            w_out = bf(ab_w_out[e])
            mixed, mix_w = [out_a, out_b], [w_out[:SWA_Q_W], w_out[SWA_Q_W:]]
        else:
            o = l // 2
            proj = _hgrn_proj(xf, norm_mix[l], bf(hgrn_w_in[o]), lower_bounds[l])
            mixed = [_hgrn_mixer_core(proj, hgrn_o_norm[o], batch, seq)]
            mix_w = [bf(hgrn_w_out[o])]
        k, v = _xa_kv(memf, norm_mem[l], bf(xa_w_kv[l]), xa_k_norm[l])
        xf = _cross_attention(xf, mixed, mix_w, norm_cross[l], bf(xa_w_q[l]), xa_q_norm[l], k, v,
                              bf(xa_w_o[l]), seq, n_mem)
        xf = _conv_ffn(xf, norm_ffn[l], bf(ffn_w_up[l]), ffn_conv_w[l], ffn_conv_b[l],
                       bf(ffn_w_down[l]), seq)
    return xf.reshape(batch, seq, d)
```

```python
import functools

import jax
import jax.numpy as jnp
from jax import lax
from jax.experimental import pallas as pl
from jax.experimental.pallas import tpu as pltpu

F32 = jnp.float32
BF16 = jnp.bfloat16

D_MODEL = 1024
HEAD_DIM = 64
SWA_HEADS = 8
SWA_KV_HEADS = 2
WINDOW = 128
SB_HEADS = 8
SB_BLOCK = 128
HGRN_HEAD_DIM = 128
HGRN_HEADS = D_MODEL // HGRN_HEAD_DIM
XA_HEADS = 4
XA_HEAD_DIM = D_MODEL // XA_HEADS
ROPE_THETA = 10000.0
EPS = 1e-6
SWA_Q_W = SWA_HEADS * HEAD_DIM
SWA_KV_W = SWA_KV_HEADS * HEAD_DIM
SB_W = SB_HEADS * HEAD_DIM
AB_IN = SWA_Q_W + 2 * SWA_KV_W + 3 * SB_W

LANES = 128
MXU_DIM = 256
TOKEN_TILE = 512
FF_CHUNK = 256
HGRN_CHUNK = 32
HGRN_STATE_CHUNK = 64
HGRN_PRE_TILE = 256
HGRN_HEADS_PER_STEP = 4
HGRN_SAFE_LOG_DECAY = -60.0
SB_Q_TILE = 512
SB_GROUP = 2
SB_EXP_ZERO = -105.0
VMEM_LIMIT = 56 * 1024 * 1024

NT_DIMS = (((1,), (1,)), ((), ()))
TN_DIMS = (((0,), (0,)), ((), ()))


def _params(*sem):
    return pltpu.CompilerParams(dimension_semantics=sem, vmem_limit_bytes=VMEM_LIMIT)


def _resident(shape):
    zeros = (0,) * len(shape)
    return pl.BlockSpec(shape, lambda *_: zeros, pipeline_mode=pl.Buffered(1))


def _rms(x, gain):
    ms = jnp.mean(x * x, axis=-1, keepdims=True)
    return x * lax.rsqrt(ms + EPS) * gain


def _split_bf16(x):
    hi = x.astype(BF16)
    lo = (x - hi.astype(F32)).astype(BF16)
    return hi, lo


def _sigmoid_pair(x):
    e = jnp.exp(-jnp.abs(x))
    big = 1.0 / (1.0 + e)
    small = e * big
    pos = x >= 0
    return jnp.where(pos, big, small), jnp.where(pos, small, big)


def _norm_matmul_kernel(tn, x_ref, g_ref, w_ref, o_ref):
    h = _rms(x_ref[...], g_ref[...]).astype(BF16)
    for j in range(w_ref.shape[1] // tn):
        cols = slice(j * tn, (j + 1) * tn)
        o_ref[:, cols] = jnp.dot(h, w_ref[:, cols], preferred_element_type=F32).astype(BF16)


def _norm_matmul(x, gain, w, tn):
    m, d = x.shape
    n = w.shape[1]
    assert n % tn == 0
    tm = TOKEN_TILE
    row = lambda i: (i, 0)
    return pl.pallas_call(
        functools.partial(_norm_matmul_kernel, tn),
        grid=(m // tm,),
        in_specs=[pl.BlockSpec((tm, d), row), pl.BlockSpec((1, d), lambda i: (0, 0)),
                  _resident(w.shape)],
        out_specs=pl.BlockSpec((tm, n), row),
        out_shape=jax.ShapeDtypeStruct((m, n), BF16),
        compiler_params=_params("arbitrary"),
    )(x, gain.reshape(1, d), w)


def _hgrn_proj_kernel(x_ref, g_ref, w_ref, lb_ref, o_ref):
    d = x_ref.shape[1]
    h = _rms(x_ref[...], g_ref[...]).astype(BF16)
    tn = MXU_DIM
    for c in range(d // tn):
        cols = slice(c * tn, (c + 1) * tn)
        proj = lambda k: jnp.dot(h, w_ref[:, k * d + c * tn:k * d + (c + 1) * tn],
                                 preferred_element_type=F32)
        sec = lambda k: slice(k * d + c * tn, k * d + (c + 1) * tn)
        lb = lb_ref[:, cols]
        sig, sig_neg = _sigmoid_pair(proj(1))
        o_ref[:, sec(0)] = proj(0).astype(BF16)
        hi, lo = _split_bf16(jnp.log(lb + (1.0 - lb) * sig))
        o_ref[:, sec(1)] = hi
        o_ref[:, sec(4)] = lo
        o_ref[:, sec(5)] = ((1.0 - lb) * sig_neg).astype(BF16)
        gate = proj(3)
        o_ref[:, sec(2)] = proj(2).astype(BF16)
        o_ref[:, sec(3)] = (gate * (1.0 / (1.0 + jnp.exp(-gate)))).astype(BF16)


def _hgrn_proj(x, gain, w, lower_bound):
    m, d = x.shape
    tm = TOKEN_TILE
    row = lambda i: (i, 0)
    const = lambda i: (0, 0)
    return pl.pallas_call(
        _hgrn_proj_kernel,
        grid=(m // tm,),
        in_specs=[pl.BlockSpec((tm, d), row), pl.BlockSpec((1, d), const), _resident(w.shape),
                  pl.BlockSpec((1, d), const)],
        out_specs=pl.BlockSpec((tm, 6 * d), row),
        out_shape=jax.ShapeDtypeStruct((m, 6 * d), BF16),
        compiler_params=_params("arbitrary"),
    )(x, gain.reshape(1, d), w, lower_bound.reshape(1, d))


def _rope_table_kernel(pos_ref, inv_ref, cos_ref, sin_ref):
    ang = pos_ref[...] * inv_ref[...]
    lane = lax.broadcasted_iota(jnp.int32, ang.shape, 1)
    first_half = (lane % HEAD_DIM) < HEAD_DIM // 2
    cos_ref[...] = jnp.cos(ang)
    sin = jnp.sin(ang)
    sin_ref[...] = jnp.where(first_half, -sin, sin)


def _rope_tables(positions):
    m = positions.size
    tm = TOKEN_TILE
    half = HEAD_DIM // 2
    inv_freq = ROPE_THETA ** (-jnp.arange(0, HEAD_DIM, 2, dtype=F32) / HEAD_DIM)
    inv = jnp.tile(inv_freq, LANES // half).reshape(1, LANES)
    pos = positions.astype(F32).reshape(m, 1)
    row = lambda i: (i, 0)
    return pl.pallas_call(
        _rope_table_kernel,
        grid=(m // tm,),
        in_specs=[pl.BlockSpec((tm, 1), row), pl.BlockSpec((1, LANES), lambda i: (0, 0))],
        out_specs=[pl.BlockSpec((tm, LANES), row), pl.BlockSpec((tm, LANES), row)],
        out_shape=[jax.ShapeDtypeStruct((m, LANES), F32)] * 2,
        compiler_params=_params("arbitrary"),
    )(pos, inv)


def _group_ones(width):
    r = lax.broadcasted_iota(jnp.int32, (width, width), 0) // HEAD_DIM
    c = lax.broadcasted_iota(jnp.int32, (width, width), 1) // HEAD_DIM
    return (r == c).astype(BF16)


def _head_norm_rope(x, gain, cos, sin):
    hi, lo = _split_bf16(x * x)
    ones = _group_ones(LANES)
    ss = (jnp.dot(hi, ones, preferred_element_type=F32)
          + jnp.dot(lo, ones, preferred_element_type=F32))
    y = x * lax.rsqrt(ss * (1.0 / HEAD_DIM) + EPS) * gain
    lane = lax.broadcasted_iota(jnp.int32, y.shape, 1)
    first_half = (lane % HEAD_DIM) < HEAD_DIM // 2
    partner = jnp.where(first_half,
                        pltpu.roll(y, LANES - HEAD_DIM // 2, 1),
                        pltpu.roll(y, HEAD_DIM // 2, 1))
    return y * cos + partner * sin


def _swa_kernel(seq, qkv_ref, cos_ref, sin_ref, qg_ref, kg_ref, sink_ref, o_ref,
                q_scr, k_scr, v_scr):
    rt = min(seq, TOKEN_TILE)
    n_q_chunks = SWA_Q_W // LANES
    for r in range(seq // rt):
        rows = slice(r * rt, (r + 1) * rt)
        cos, sin = cos_ref[rows, :], sin_ref[rows, :]
        for c in range(n_q_chunks):
            cols = slice(c * LANES, (c + 1) * LANES)
            xq = qkv_ref[rows, cols].astype(F32)
            q = _head_norm_rope(xq, qg_ref[...], cos, sin)
            q_scr[c, rows, :] = (q * HEAD_DIM ** -0.5).astype(BF16)
        xk = qkv_ref[rows, SWA_Q_W:SWA_Q_W + SWA_KV_W].astype(F32)
        k = _head_norm_rope(xk, kg_ref[...], cos, sin)
        v = qkv_ref[rows, SWA_Q_W + SWA_KV_W:SWA_Q_W + 2 * SWA_KV_W].astype(F32)
        low = lax.broadcasted_iota(jnp.int32, k.shape, 1) < HEAD_DIM
        for t, scr in ((k, k_scr), (v, v_scr)):
            swapped = pltpu.roll(t, HEAD_DIM, 1)
            scr[0, 0, rows, :] = jnp.where(low, t, 0.0).astype(BF16)
            scr[1, 0, rows, :] = jnp.where(low, 0.0, swapped).astype(BF16)
            scr[0, 1, rows, :] = jnp.where(low, swapped, 0.0).astype(BF16)
            scr[1, 1, rows, :] = jnp.where(low, 0.0, t).astype(BF16)

    w = WINDOW
    qi = lax.broadcasted_iota(jnp.int32, (2 * w, w), 0) % w
    kj = lax.broadcasted_iota(jnp.int32, (2 * w, w), 1)
    top = lax.broadcasted_iota(jnp.int32, (2 * w, 1), 0) < w
    cur_mask = kj <= qi

    def block(n, carry):
        r0 = pl.multiple_of(n * w, w)
        p0 = pl.multiple_of(jnp.maximum(n - 1, 0) * w, w)
        prev_mask = (kj > qi) & (n > 0)
        for g in range(SWA_KV_HEADS):
            qa = jnp.concatenate([q_scr[2 * g, pl.ds(r0, w), :],
                                  q_scr[2 * g + 1, pl.ds(r0, w), :]], axis=0)
            out = jnp.zeros((2 * w, LANES), F32)
            for slot in range(2):
                s_cur = lax.dot_general(qa, k_scr[slot, g, pl.ds(r0, w), :], NT_DIMS,
                                        preferred_element_type=F32)
                s_prev = lax.dot_general(qa, k_scr[slot, g, pl.ds(p0, w), :], NT_DIMS,
                                         preferred_element_type=F32)
                s_cur = jnp.where(cur_mask, s_cur, -jnp.inf)
                s_prev = jnp.where(prev_mask, s_prev, -jnp.inf)
                sink = jnp.where(top, sink_ref[4 * g + slot], sink_ref[4 * g + 2 + slot])
                m = jnp.maximum(jnp.max(jnp.maximum(s_cur, s_prev), axis=-1, keepdims=True), sink)
                p_cur = jnp.exp(s_cur - m)
                p_prev = jnp.exp(s_prev - m)
                den = jnp.sum(p_cur + p_prev, axis=-1, keepdims=True) + jnp.exp(sink - m)
                inv = 1.0 / den
                out = out + jnp.dot((p_cur * inv).astype(BF16), v_scr[slot, g, pl.ds(r0, w), :],
                                    preferred_element_type=F32)
                out = out + jnp.dot((p_prev * inv).astype(BF16), v_scr[slot, g, pl.ds(p0, w), :],
                                    preferred_element_type=F32)
            o_ref[pl.ds(r0, w), 2 * g * LANES:(2 * g + 1) * LANES] = out[:w].astype(BF16)
            o_ref[pl.ds(r0, w), (2 * g + 1) * LANES:(2 * g + 2) * LANES] = out[w:].astype(BF16)
        return carry

    lax.fori_loop(0, seq // w, block, 0, unroll=8)


def _swa_attention(qkv, cos, sin, q_gain, k_gain, sinks, batch, seq):
    width = SWA_Q_W + 2 * SWA_KV_W
    assert AB_IN % width == 0
    const = lambda b: (0, 0)
    return pl.pallas_call(
        functools.partial(_swa_kernel, seq),
        grid=(batch,),
        in_specs=[
            pl.BlockSpec((seq, width), lambda b: (b, 0)),
            pl.BlockSpec((seq, LANES), lambda b: (b, 0)),
            pl.BlockSpec((seq, LANES), lambda b: (b, 0)),
            pl.BlockSpec((1, LANES), const),
            pl.BlockSpec((1, LANES), const),
            pl.BlockSpec(memory_space=pltpu.SMEM),
        ],
        out_specs=pl.BlockSpec((seq, SWA_Q_W), lambda b: (b, 0)),
        out_shape=jax.ShapeDtypeStruct((batch * seq, SWA_Q_W), BF16),
        scratch_shapes=[
            pltpu.VMEM((SWA_Q_W // LANES, seq, LANES), BF16),
            pltpu.VMEM((2, SWA_KV_HEADS, seq, LANES), BF16),
            pltpu.VMEM((2, SWA_KV_HEADS, seq, LANES), BF16),
        ],
        compiler_params=_params("arbitrary"),
    )(qkv, cos, sin,
      jnp.tile(q_gain, LANES // HEAD_DIM).reshape(1, LANES),
      jnp.tile(k_gain, LANES // HEAD_DIM).reshape(1, LANES),
      sinks)


def _neg_abs(x):
    bits = lax.bitcast_convert_type(x, jnp.uint32) | jnp.uint32(0x80000000)
    return lax.bitcast_convert_type(bits, F32)


def _sb_kernel(seq, q_ref, k_ref, v_ref, o_ref, k_scr, v_scr, later_scr, causal_scr, acc_ref,
               carry_ref):
    blk = SB_BLOCK
    tq = min(seq, SB_Q_TILE)
    per_tile = tq // blk
    nb = seq // blk
    low = lax.broadcasted_iota(jnp.int32, (seq, LANES), 1) < HEAD_DIM
    for src, scr in ((k_ref, k_scr), (v_ref, v_scr)):
        full = src[...]
        zero = jnp.zeros_like(full)
        scr[:, 0:blk, :] = jnp.where(low, full, zero).reshape(nb, blk, LANES)
        scr[:, blk:2 * blk, :] = jnp.where(low, zero, full).reshape(nb, blk, LANES)

    r = lax.broadcasted_iota(jnp.int32, (2 * blk, 2 * blk), 0)
    c = lax.broadcasted_iota(jnp.int32, (2 * blk, 2 * blk), 1)
    later_scr[...] = ((r // blk == c // blk) & (r % blk > c % blk)).astype(BF16)
    causal_scr[...] = (lax.broadcasted_iota(jnp.int32, (tq, 2 * blk), 1) % blk
                       < lax.broadcasted_iota(jnp.int32, (tq, 2 * blk), 0)).astype(F32)

    def scores(q_rows, n_rows, kb, mask):
        qt = q_ref[pl.ds(q_rows, n_rows), :] * jnp.asarray(HEAD_DIM ** -0.5, BF16)
        z = lax.dot_general(qt, k_scr[kb], NT_DIMS, preferred_element_type=F32)
        log1p_e = jnp.log(1.0 + jnp.exp(_neg_abs(z)))
        log_beta = jnp.minimum(z, 0.0) - log1p_e
        log_fail = log_beta - z
        if mask is not None:
            log_fail = log_fail * mask
        after = jnp.dot(log_fail.astype(BF16), later_scr[...], preferred_element_type=F32)
        totals = [after[:, h * blk:h * blk + 1] + log_fail[:, h * blk:h * blk + 1]
                  for h in range(2)]
        return log_beta + after, totals

    def weights(parts, carry, mask):
        log_w, totals = parts
        a = jnp.exp(log_w + carry)
        if mask is not None:
            a = a * mask
        halves = [carry[:, h * blk:(h + 1) * blk] + totals[h] for h in range(2)]
        return a.astype(BF16), jnp.concatenate(halves, axis=1)

    def tile(qi, carry_):
        row0 = pl.multiple_of(qi * tq, tq)
        first = qi * per_tile
        order = list(reversed(range(per_tile)))
        masks = {j: causal_scr[0:tq - j * blk, :] for j in order}
        parts = {j: scores(row0 + j * blk, tq - j * blk, first + j, masks[j]) for j in order}
        carry = jnp.zeros((tq, 2 * blk), F32)
        pv = jnp.zeros((tq, LANES), F32)
        for j in order:
            r0 = j * blk
            a, tail = weights(parts[j], carry[r0:], masks[j])
            upd = pv[r0:] + jnp.dot(a, v_scr[first + j], preferred_element_type=F32)
            carry = tail if r0 == 0 else jnp.concatenate([carry[:r0], tail], axis=0)
            pv = upd if r0 == 0 else jnp.concatenate([pv[:r0], upd], axis=0)
        acc_ref[...] = pv
        carry_ref[...] = carry

        def below(state):
            i, _ = state
            kbs = [first - 1 - (SB_GROUP * i + u) for u in range(SB_GROUP)]
            parts = [scores(row0, tq, kb, None) for kb in kbs]
            carry = carry_ref[...]
            pv = jnp.zeros((tq, LANES), F32)
            for kb, p in zip(kbs, parts):
                a, carry = weights(p, carry, None)
                pv = pv + jnp.dot(a, v_scr[kb], preferred_element_type=F32)
            acc_ref[...] += pv
            carry_ref[...] = carry
            return i + 1, jnp.max(carry)

        lax.while_loop(lambda s: (s[0] < qi * (per_tile // SB_GROUP)) & (s[1] > SB_EXP_ZERO), below,
                       (jnp.int32(0), jnp.float32(0.0)))
        o_ref[pl.ds(row0, tq), :] = acc_ref[...].astype(BF16)
        return carry_

    lax.fori_loop(0, seq // tq, tile, 0)


def _sb_attention(qkv, batch, seq):
    tq = min(seq, SB_Q_TILE)
    pairs = SB_W // LANES
    q0 = (SWA_Q_W + 2 * SWA_KV_W) // LANES
    k0, v0 = q0 + pairs, q0 + 2 * pairs
    return pl.pallas_call(
        functools.partial(_sb_kernel, seq),
        grid=(batch, pairs),
        in_specs=[
            pl.BlockSpec((seq, LANES), lambda b, p: (b, q0 + p)),
            pl.BlockSpec((seq, LANES), lambda b, p: (b, k0 + p)),
            pl.BlockSpec((seq, LANES), lambda b, p: (b, v0 + p)),
        ],
        out_specs=pl.BlockSpec((seq, LANES), lambda b, p: (b, p)),
        out_shape=jax.ShapeDtypeStruct((batch * seq, SB_W), BF16),
        scratch_shapes=[
            pltpu.VMEM((seq // SB_BLOCK, 2 * SB_BLOCK, LANES), BF16),
            pltpu.VMEM((seq // SB_BLOCK, 2 * SB_BLOCK, LANES), BF16),
            pltpu.VMEM((2 * SB_BLOCK, 2 * SB_BLOCK), BF16),
            pltpu.VMEM((tq, 2 * SB_BLOCK), F32),
            pltpu.VMEM((tq, LANES), F32),
            pltpu.VMEM((tq, 2 * SB_BLOCK), F32),
        ],
        compiler_params=_params("arbitrary", "arbitrary"),
    )(qkv, qkv, qkv)


def _hgrn_kernel(seq, q_ref, hi_ref, i_ref, g_ref, lo_ref, k_ref, on_ref, o_ref,
                 qs_scr, ks_scr, b_scr, o_scr, dec_scr, st_scr, qrow_scr):
    ch = HGRN_CHUNK
    sc = HGRN_STATE_CHUNK
    hd = HGRN_HEAD_DIM
    heads = HGRN_HEADS_PER_STEP
    width = heads * hd
    pt = min(seq, HGRN_PRE_TILE)

    r = lax.broadcasted_iota(jnp.int32, (pt, pt), 0)
    c = lax.broadcasted_iota(jnp.int32, (pt, pt), 1)
    same_sub = (r // ch) == (c // ch)
    same_chunk = (r // sc) == (c // sc)
    m_diag = same_sub & (c <= r)
    m_prev = same_chunk & ((r // ch) == (c // ch) + 1)
    twice = lambda m: jnp.concatenate([m.astype(BF16)] * 2, axis=1)
    cum_sub, tot_sub, tot_chunk = twice(m_diag), twice(same_sub), twice(same_chunk)
    first_sub = (lax.broadcasted_iota(jnp.int32, (pt, 1), 0) // ch) % (sc // ch) == 0
    s_idx = lax.broadcasted_iota(jnp.int32, (ch, 1), 0)
    row8 = lax.broadcasted_iota(jnp.int32, (8, 1), 0)

    def exact_diagonal(row0):
        def sub_chunk(n, carry):
            r0 = pl.multiple_of(row0 + n * ch, ch)
            for h in range(heads):
                cols = slice(h * hd, (h + 1) * hd)
                b_c = b_scr[pl.ds(r0, ch), cols]
                k_c = k_ref[pl.ds(r0, ch), cols].astype(F32)
                v_c = i_ref[pl.ds(r0, ch), cols].astype(F32)
                qrow_scr[...] = q_ref[pl.ds(r0, ch), cols].astype(F32)

                def row(t, cc, b_c=b_c, k_c=k_c, v_c=v_c, cols=cols):
                    g0 = pl.multiple_of((t // 8) * 8, 8)
                    pick = row8 == (t % 8)
                    take = lambda blk: jnp.sum(jnp.where(pick, blk, 0.0), axis=0, keepdims=True)
                    b_t = take(b_scr[pl.ds(r0 + g0, 8), cols])
                    q_t = take(qrow_scr[pl.ds(g0, 8), :])
                    w = jnp.where(s_idx <= t, jnp.exp(jnp.minimum(b_t - b_c, 0.0)), 0.0)
                    sc_t = jnp.sum(q_t * k_c * w, axis=-1, keepdims=True)
                    o_t = jnp.sum(sc_t * v_c, axis=0, keepdims=True)
                    o_scr[pl.ds(r0 + g0, 8), cols] += jnp.where(pick, o_t, 0.0)
                    return cc

                lax.fori_loop(0, ch, row, 0)
            return carry

        lax.fori_loop(0, pt // ch, sub_chunk, 0)

    def gates(t):
        rows = slice(t * pt, (t + 1) * pt)
        return (k_ref[rows, :].astype(F32),
                jnp.concatenate([hi_ref[rows, :], lo_ref[rows, :]], axis=0))

    def decays(parts):
        return [jnp.dot(m, parts, preferred_element_type=F32)
                for m in (cum_sub, tot_sub, tot_chunk)]

    def factors(t, kk, b, end_sub, end_chunk):
        rows = slice(t * pt, (t + 1) * pt)
        before = jnp.where(first_sub, 0.0, end_chunk - end_sub)
        qd = q_ref[rows, :].astype(F32) * jnp.exp(b)
        qs_scr[rows, :] = (qd * jnp.exp(before)).astype(BF16)
        ks_scr[rows, :] = (kk * jnp.exp(end_chunk - b - before)).astype(BF16)
        b_scr[rows, :] = b
        for i in range(pt // sc):
            dec_scr[t * (pt // sc) + i] = jnp.broadcast_to(
                jnp.exp(end_chunk[i * sc:i * sc + 1, :]), (8, width))
        safe = jnp.min(end_sub) > HGRN_SAFE_LOG_DECAY
        return (qd.astype(BF16), (kk * jnp.exp(-b)).astype(BF16),
                (kk * jnp.exp(end_sub - b)).astype(BF16), safe)

    def intra(t, qd_b, kd_b, ke_b, safe):
        rows = slice(t * pt, (t + 1) * pt)
        v = i_ref[rows, :]
        for h in range(heads):
            cols = slice(h * hd, (h + 1) * hd)
            a_diag = lax.dot_general(qd_b[:, cols], kd_b[:, cols], NT_DIMS,
                                     preferred_element_type=F32)
            a_prev = lax.dot_general(qd_b[:, cols], ke_b[:, cols], NT_DIMS,
                                     preferred_element_type=F32)
            a = jnp.where(m_diag, jnp.where(safe, a_diag, 0.0), jnp.where(m_prev, a_prev, 0.0))
            o_scr[rows, cols] = jnp.dot(a.astype(BF16), v[:, cols], preferred_element_type=F32)

    n_tiles = seq // pt
    g_out, d_out, f_out = {}, {}, {}
    for step in range(n_tiles + 3):
        if step < n_tiles:
            g_out[step] = gates(step)
        if 0 <= step - 1 < n_tiles:
            d_out[step - 1] = decays(g_out[step - 1][1])
        if 0 <= step - 2 < n_tiles:
            f_out[step - 2] = factors(step - 2, g_out[step - 2][0], *d_out[step - 2])
        if 0 <= step - 3 < n_tiles:
            intra(step - 3, *f_out[step - 3])

    for t in range(n_tiles):
        @pl.when(jnp.logical_not(f_out[t][3]))
        def _(t=t):
            exact_diagonal(t * pt)

    st_scr[...] = jnp.zeros_like(st_scr)

    def state_step(n, carry):
        r0 = pl.multiple_of(n * sc, sc)
        decay = dec_scr[n]
        for h in range(heads):
            cols = slice(h * hd, (h + 1) * hd)
            st = st_scr[h]
            o_scr[pl.ds(r0, sc), cols] += lax.dot_general(
                qs_scr[pl.ds(r0, sc), cols], st.astype(BF16), NT_DIMS, preferred_element_type=F32)
            kv = lax.dot_general(i_ref[pl.ds(r0, sc), cols], ks_scr[pl.ds(r0, sc), cols], TN_DIMS,
                                 preferred_element_type=F32)
            st_scr[h] = st * decay[0:1, cols] + kv
        return carry

    lax.fori_loop(0, seq // sc, state_step, 0, unroll=8)

    for t in range(seq // pt):
        rows = slice(t * pt, (t + 1) * pt)
        silu = g_ref[rows, :].astype(F32)
        for h in range(heads):
            cols = slice(h * hd, (h + 1) * hd)
            y = _rms(o_scr[rows, cols], on_ref[...])
            o_ref[rows, cols] = (y * silu[:, cols]).astype(BF16)


def _hgrn_mixer_core(proj, o_gain, batch, seq):
    heads = HGRN_HEADS_PER_STEP
    width = heads * HGRN_HEAD_DIM
    groups = HGRN_HEADS // heads
    assert seq % HGRN_PRE_TILE == 0 and HGRN_PRE_TILE % HGRN_STATE_CHUNK == 0
    section = lambda k: pl.BlockSpec((seq, width), lambda b, g, k=k: (b, k * groups + g))
    return pl.pallas_call(
        functools.partial(_hgrn_kernel, seq),
        grid=(batch, groups),
        in_specs=[section(k) for k in range(6)]
        + [pl.BlockSpec((1, HGRN_HEAD_DIM), lambda b, g: (0, 0))],
        out_specs=pl.BlockSpec((seq, width), lambda b, g: (b, g)),
        out_shape=jax.ShapeDtypeStruct((batch * seq, D_MODEL), BF16),
        scratch_shapes=[
            pltpu.VMEM((seq, width), BF16),
            pltpu.VMEM((seq, width), BF16),
            pltpu.VMEM((seq, width), F32),
            pltpu.VMEM((seq, width), F32),
            pltpu.VMEM((seq // HGRN_STATE_CHUNK, 8, width), F32),
            pltpu.VMEM((heads, HGRN_HEAD_DIM, HGRN_HEAD_DIM), F32),
            pltpu.VMEM((HGRN_CHUNK, HGRN_HEAD_DIM), F32),
        ],
        compiler_params=_params("arbitrary", "arbitrary"),
    )(*([proj] * 6), o_gain.reshape(1, HGRN_HEAD_DIM))


def _xa_kv_kernel(mem_ref, g_ref, w_ref, kn_ref, k_ref, v_ref):
    mem_n = _rms(mem_ref[...], g_ref[...]).astype(BF16)
    kv = jnp.dot(mem_n, w_ref[...], preferred_element_type=F32)
    for h in range(XA_HEADS):
        cols = slice(h * XA_HEAD_DIM, (h + 1) * XA_HEAD_DIM)
        k_ref[:, cols] = _rms(kv[:, cols], kn_ref[...]).astype(BF16)
    v_ref[...] = kv[:, D_MODEL:].astype(BF16)


def _xa_kv(mem, gain, w_kv, k_gain):
    m, d = mem.shape
    tm = min(m, TOKEN_TILE)
    row = lambda i: (i, 0)
    return pl.pallas_call(
        _xa_kv_kernel,
        grid=(m // tm,),
        in_specs=[pl.BlockSpec((tm, d), row), pl.BlockSpec((1, d), lambda i: (0, 0)),
                  _resident(w_kv.shape), pl.BlockSpec((1, XA_HEAD_DIM), lambda i: (0, 0))],
        out_specs=[pl.BlockSpec((tm, d), row), pl.BlockSpec((tm, d), row)],
        out_shape=[jax.ShapeDtypeStruct((m, d), BF16)] * 2,
        compiler_params=_params("arbitrary"),
    )(mem, gain.reshape(1, d), w_kv, k_gain.reshape(1, XA_HEAD_DIM))


def _xa_kernel(n_in, x_ref, *refs):
    a_refs, w_refs = refs[:n_in], refs[n_in:2 * n_in]
    g_ref, wq_ref, qn_ref, k_ref, v_ref, wo_ref, o_ref = refs[2 * n_in:]
    x = x_ref[...]
    for a_ref, w_ref in zip(a_refs, w_refs):
        x = x + jnp.dot(a_ref[...], w_ref[...], preferred_element_type=F32)
    h = _rms(x, g_ref[...]).astype(BF16)
    q = jnp.dot(h, wq_ref[...], preferred_element_type=F32)
    heads = []
    for hd in range(XA_HEADS):
        cols = slice(hd * XA_HEAD_DIM, (hd + 1) * XA_HEAD_DIM)
        qh = (_rms(q[:, cols], qn_ref[...]) * XA_HEAD_DIM ** -0.5).astype(BF16)
        s = lax.dot_general(qh, k_ref[:, cols], NT_DIMS, preferred_element_type=F32)
        p = jnp.exp(s - jnp.max(s, axis=-1, keepdims=True))
        p = p * (1.0 / jnp.sum(p, axis=-1, keepdims=True))
        heads.append(jnp.dot(p.astype(BF16), v_ref[:, cols],
                             preferred_element_type=F32).astype(BF16))
    o = jnp.concatenate(heads, axis=1)
    o_ref[...] = x + jnp.dot(o, wo_ref[...], preferred_element_type=F32)


def _cross_attention(x, acts, weights, gain, w_q, q_gain, k, v, w_o, seq, n_mem):
    m, d = x.shape
    tm = min(seq, TOKEN_TILE)
    per_seq = seq // tm
    row = lambda i: (i, 0)
    mem_row = lambda i: (i // per_seq, 0)
    return pl.pallas_call(
        functools.partial(_xa_kernel, len(acts)),
        grid=(m // tm,),
        in_specs=[pl.BlockSpec((tm, d), row)]
        + [pl.BlockSpec((tm, a.shape[1]), row) for a in acts]
        + [_resident(w.shape) for w in weights]
        + [pl.BlockSpec((1, d), lambda i: (0, 0)),
           _resident(w_q.shape), pl.BlockSpec((1, XA_HEAD_DIM), lambda i: (0, 0)),
           pl.BlockSpec((n_mem, d), mem_row), pl.BlockSpec((n_mem, d), mem_row),
           _resident(w_o.shape)],
        out_specs=pl.BlockSpec((tm, d), row),
        out_shape=jax.ShapeDtypeStruct((m, d), F32),
        compiler_params=_params("arbitrary"),
    )(x, *acts, *weights, gain.reshape(1, d), w_q, q_gain.reshape(1, XA_HEAD_DIM), k, v, w_o)


def _ffn_kernel(per_seq, d_ff, x_ref, g_ref, wup_ref, cw_ref, cb_ref, wdn_ref, o_ref,
                halo_ref):
    tm = x_ref.shape[0]
    fc = FF_CHUNK
    n_chunks = d_ff // fc
    groups = tm // 8

    @pl.when(pl.program_id(0) % per_seq == 0)
    def _():
        halo_ref[...] = jnp.zeros_like(halo_ref)

    x = x_ref[...]
    h = _rms(x, g_ref[...]).astype(BF16)
    sub = lax.broadcasted_iota(jnp.int32, (groups, 8, fc), 1)


    def up_proj(j):
        return [jnp.dot(h, wup_ref[:, c0:c0 + fc], preferred_element_type=F32)
                for c0 in (j * fc, d_ff + j * fc)]

    def conv(u, col0, slot):
        cols = slice(col0, col0 + fc)
        prev = halo_ref[slot]
        halo_ref[slot] = u[tm - 8:, :]
        grouped = jnp.concatenate([prev, u], axis=0).reshape(groups + 1, 8, fc)
        out = cw_ref[2:3, cols] * u + cb_ref[:, cols]
        for n in (1, 2):
            rot = pltpu.roll(grouped, n, 1)
            shifted = jnp.where(sub < n, rot[:groups], rot[1:]).reshape(tm, fc)
            out = out + cw_ref[2 - n:3 - n, cols] * shifted
        return out

    nxt = up_proj(0)
    acts = []
    for j in range(n_chunks):
        cur = nxt
        if j + 1 < n_chunks:
            nxt = up_proj(j + 1)
        gate = conv(cur[0], j * fc, j)
        up = conv(cur[1], d_ff + j * fc, n_chunks + j)
        acts.append((gate * (1.0 / (1.0 + jnp.exp(-gate))) * up).astype(BF16))
    o_ref[...] = x + jnp.dot(jnp.concatenate(acts, axis=1), wdn_ref[...],
                             preferred_element_type=F32)


def _conv_ffn(x, gain, w_up, conv_w, conv_b, w_down, seq):
    m, d = x.shape
    d_ff = w_down.shape[0]
    assert d_ff % FF_CHUNK == 0
    tm = min(seq, TOKEN_TILE)
    row = lambda i: (i, 0)
    const = lambda i: (0, 0)
    return pl.pallas_call(
        functools.partial(_ffn_kernel, seq // tm, d_ff),
        grid=(m // tm,),
        in_specs=[pl.BlockSpec((tm, d), row), pl.BlockSpec((1, d), const),
                  _resident(w_up.shape), pl.BlockSpec(conv_w.shape, const),
                  pl.BlockSpec((1, 2 * d_ff), const), _resident(w_down.shape)],
        out_specs=pl.BlockSpec((tm, d), row),
        out_shape=jax.ShapeDtypeStruct((m, d), F32),
        scratch_shapes=[pltpu.VMEM((2 * d_ff // FF_CHUNK, 8, FF_CHUNK), F32)],
        compiler_params=_params("arbitrary"),
    )(x, gain.reshape(1, d), w_up, conv_w, conv_b.reshape(1, 2 * d_ff), w_down)


def kernel(x, mem, positions, norm_mix, norm_cross, norm_mem, norm_ffn, ab_w_in, ab_w_out, swa_q_norm, swa_k_norm, swa_sinks, hgrn_w_in, hgrn_w_out, hgrn_o_norm, hgrn_lb, xa_w_q, xa_w_kv, xa_w_o, xa_q_norm, xa_k_norm, ffn_w_up, ffn_conv_w, ffn_conv_b, ffn_w_down):
    batch, seq, d = x.shape
    n_mem = mem.shape[1]
    depth = norm_mix.shape[0]
    assert d == D_MODEL and seq % TOKEN_TILE == 0
    assert (batch * n_mem) % min(batch * n_mem, TOKEN_TILE) == 0
    bf = lambda w: w.astype(BF16)

    p_lb = jax.nn.softmax(hgrn_lb.astype(F32), axis=0)
    lower_bounds = jnp.cumsum(p_lb, axis=0) - p_lb[0]

    xf = x.reshape(batch * seq, d)
    memf = mem.reshape(batch * n_mem, d)
    cos, sin = _rope_tables(positions)

    for l in range(depth):
        if l % 2 == 0:
            e = l // 2
            qkv = _norm_matmul(xf, norm_mix[l], bf(ab_w_in[e]), tn=AB_IN // 3)
            out_a = _swa_attention(qkv, cos, sin, swa_q_norm[e], swa_k_norm[e], swa_sinks[e],
                                   batch, seq)
            out_b = _sb_attention(qkv, batch, seq)
            w_out = bf(ab_w_out[e])
            mixed, mix_w = [out_a, out_b], [w_out[:SWA_Q_W], w_out[SWA_Q_W:]]
        else:
            o = l // 2
            proj = _hgrn_proj(xf, norm_mix[l], bf(hgrn_w_in[o]), lower_bounds[l])
            mixed = [_hgrn_mixer_core(proj, hgrn_o_norm[o], batch, seq)]
            mix_w = [bf(hgrn_w_out[o])]
        k, v = _xa_kv(memf, norm_mem[l], bf(xa_w_kv[l]), xa_k_norm[l])
        xf = _cross_attention(xf, mixed, mix_w, norm_cross[l], bf(xa_w_q[l]), xa_q_norm[l], k, v,
                              bf(xa_w_o[l]), seq, n_mem)
        xf = _conv_ffn(xf, norm_ffn[l], bf(ffn_w_up[l]), ffn_conv_w[l], ffn_conv_b[l],
                       bf(ffn_w_down[l]), seq)
    return xf.reshape(batch, seq, d)
```

```python
import functools

import jax
import jax.numpy as jnp
from jax import lax
from jax.experimental import pallas as pl
from jax.experimental.pallas import tpu as pltpu

F32 = jnp.float32
BF16 = jnp.bfloat16

D_MODEL = 1024
HEAD_DIM = 64
SWA_HEADS = 8
SWA_KV_HEADS = 2
WINDOW = 128
SB_HEADS = 8
SB_BLOCK = 128
HGRN_HEAD_DIM = 128
HGRN_HEADS = D_MODEL // HGRN_HEAD_DIM
XA_HEADS = 4
XA_HEAD_DIM = D_MODEL // XA_HEADS
ROPE_THETA = 10000.0
EPS = 1e-6
SWA_Q_W = SWA_HEADS * HEAD_DIM
SWA_KV_W = SWA_KV_HEADS * HEAD_DIM
SB_W = SB_HEADS * HEAD_DIM
AB_IN = SWA_Q_W + 2 * SWA_KV_W + 3 * SB_W

LANES = 128
MXU_DIM = 256
TOKEN_TILE = 512
FF_CHUNK = 256
HGRN_CHUNK = 32
HGRN_STATE_CHUNK = 64
HGRN_PRE_TILE = 256
HGRN_HEADS_PER_STEP = 4
HGRN_SAFE_LOG_DECAY = -60.0
SB_Q_TILE = 512
SB_GROUP = 2
SB_EXP_ZERO = -105.0
VMEM_LIMIT = 56 * 1024 * 1024

NT_DIMS = (((1,), (1,)), ((), ()))
TN_DIMS = (((0,), (0,)), ((), ()))


def _params(*sem):
    return pltpu.CompilerParams(dimension_semantics=sem, vmem_limit_bytes=VMEM_LIMIT)


def _resident(shape):
    zeros = (0,) * len(shape)
    return pl.BlockSpec(shape, lambda *_: zeros, pipeline_mode=pl.Buffered(1))


def _rms(x, gain):
    ms = jnp.mean(x * x, axis=-1, keepdims=True)
    return x * lax.rsqrt(ms + EPS) * gain


def _split_bf16(x):
    hi = x.astype(BF16)
    lo = (x - hi.astype(F32)).astype(BF16)
    return hi, lo


def _sigmoid_pair(x):
    e = jnp.exp(-jnp.abs(x))
    big = 1.0 / (1.0 + e)
    small = e * big
    pos = x >= 0
    return jnp.where(pos, big, small), jnp.where(pos, small, big)


def _norm_matmul_kernel(tn, x_ref, g_ref, w_ref, o_ref):
    h = _rms(x_ref[...], g_ref[...]).astype(BF16)
    for j in range(w_ref.shape[1] // tn):
        cols = slice(j * tn, (j + 1) * tn)
        o_ref[:, cols] = jnp.dot(h, w_ref[:, cols], preferred_element_type=F32).astype(BF16)


def _norm_matmul(x, gain, w, tn):
    m, d = x.shape
    n = w.shape[1]
    assert n % tn == 0
    tm = TOKEN_TILE
    row = lambda i: (i, 0)
    return pl.pallas_call(
        functools.partial(_norm_matmul_kernel, tn),
        grid=(m // tm,),
        in_specs=[pl.BlockSpec((tm, d), row), pl.BlockSpec((1, d), lambda i: (0, 0)),
                  _resident(w.shape)],
        out_specs=pl.BlockSpec((tm, n), row),
        out_shape=jax.ShapeDtypeStruct((m, n), BF16),
        compiler_params=_params("arbitrary"),
    )(x, gain.reshape(1, d), w)


def _hgrn_proj_kernel(x_ref, g_ref, w_ref, lb_ref, o_ref):
    d = x_ref.shape[1]
    h = _rms(x_ref[...], g_ref[...]).astype(BF16)
    tn = MXU_DIM
    for c in range(d // tn):
        cols = slice(c * tn, (c + 1) * tn)
        proj = lambda k: jnp.dot(h, w_ref[:, k * d + c * tn:k * d + (c + 1) * tn],
                                 preferred_element_type=F32)
        sec = lambda k: slice(k * d + c * tn, k * d + (c + 1) * tn)
        lb = lb_ref[:, cols]
        sig, sig_neg = _sigmoid_pair(proj(1))
        o_ref[:, sec(0)] = proj(0).astype(BF16)
        hi, lo = _split_bf16(jnp.log(lb + (1.0 - lb) * sig))
        o_ref[:, sec(1)] = hi
        o_ref[:, sec(4)] = lo
        o_ref[:, sec(5)] = ((1.0 - lb) * sig_neg).astype(BF16)
        gate = proj(3)
        o_ref[:, sec(2)] = proj(2).astype(BF16)
        o_ref[:, sec(3)] = (gate * (1.0 / (1.0 + jnp.exp(-gate)))).astype(BF16)


def _hgrn_proj(x, gain, w, lower_bound):
    m, d = x.shape
    tm = TOKEN_TILE
    row = lambda i: (i, 0)
    const = lambda i: (0, 0)
    return pl.pallas_call(
        _hgrn_proj_kernel,
        grid=(m // tm,),
        in_specs=[pl.BlockSpec((tm, d), row), pl.BlockSpec((1, d), const), _resident(w.shape),
                  pl.BlockSpec((1, d), const)],
        out_specs=pl.BlockSpec((tm, 6 * d), row),
        out_shape=jax.ShapeDtypeStruct((m, 6 * d), BF16),
        compiler_params=_params("arbitrary"),
    )(x, gain.reshape(1, d), w, lower_bound.reshape(1, d))


def _rope_table_kernel(pos_ref, inv_ref, cos_ref, sin_ref):
    ang = pos_ref[...] * inv_ref[...]
    lane = lax.broadcasted_iota(jnp.int32, ang.shape, 1)
    first_half = (lane % HEAD_DIM) < HEAD_DIM // 2
    cos_ref[...] = jnp.cos(ang)
    sin = jnp.sin(ang)
    sin_ref[...] = jnp.where(first_half, -sin, sin)


def _rope_tables(positions):
    m = positions.size
    tm = TOKEN_TILE
    half = HEAD_DIM // 2
    inv_freq = ROPE_THETA ** (-jnp.arange(0, HEAD_DIM, 2, dtype=F32) / HEAD_DIM)
    inv = jnp.tile(inv_freq, LANES // half).reshape(1, LANES)
    pos = positions.astype(F32).reshape(m, 1)
    row = lambda i: (i, 0)
    return pl.pallas_call(
        _rope_table_kernel,
        grid=(m // tm,),
        in_specs=[pl.BlockSpec((tm, 1), row), pl.BlockSpec((1, LANES), lambda i: (0, 0))],
        out_specs=[pl.BlockSpec((tm, LANES), row), pl.BlockSpec((tm, LANES), row)],
        out_shape=[jax.ShapeDtypeStruct((m, LANES), F32)] * 2,
        compiler_params=_params("arbitrary"),
    )(pos, inv)


def _group_ones(width):
    r = lax.broadcasted_iota(jnp.int32, (width, width), 0) // HEAD_DIM
    c = lax.broadcasted_iota(jnp.int32, (width, width), 1) // HEAD_DIM
    return (r == c).astype(BF16)


def _head_norm_rope(x, gain, cos, sin):
    hi, lo = _split_bf16(x * x)
    ones = _group_ones(LANES)
    ss = (jnp.dot(hi, ones, preferred_element_type=F32)
          + jnp.dot(lo, ones, preferred_element_type=F32))
    y = x * lax.rsqrt(ss * (1.0 / HEAD_DIM) + EPS) * gain
    lane = lax.broadcasted_iota(jnp.int32, y.shape, 1)
    first_half = (lane % HEAD_DIM) < HEAD_DIM // 2
    partner = jnp.where(first_half,
                        pltpu.roll(y, LANES - HEAD_DIM // 2, 1),
                        pltpu.roll(y, HEAD_DIM // 2, 1))
    return y * cos + partner * sin


def _swa_kernel(seq, qkv_ref, cos_ref, sin_ref, qg_ref, kg_ref, sink_ref, o_ref,
                q_scr, k_scr, v_scr):
    rt = min(seq, TOKEN_TILE)
    n_q_chunks = SWA_Q_W // LANES
    for r in range(seq // rt):
        rows = slice(r * rt, (r + 1) * rt)
        cos, sin = cos_ref[rows, :], sin_ref[rows, :]
        for c in range(n_q_chunks):
            cols = slice(c * LANES, (c + 1) * LANES)
            xq = qkv_ref[rows, cols].astype(F32)
            q = _head_norm_rope(xq, qg_ref[...], cos, sin)
            q_scr[c, rows, :] = (q * HEAD_DIM ** -0.5).astype(BF16)
        xk = qkv_ref[rows, SWA_Q_W:SWA_Q_W + SWA_KV_W].astype(F32)
        k = _head_norm_rope(xk, kg_ref[...], cos, sin)
        v = qkv_ref[rows, SWA_Q_W + SWA_KV_W:SWA_Q_W + 2 * SWA_KV_W].astype(F32)
        low = lax.broadcasted_iota(jnp.int32, k.shape, 1) < HEAD_DIM
        for t, scr in ((k, k_scr), (v, v_scr)):
            swapped = pltpu.roll(t, HEAD_DIM, 1)
            scr[0, 0, rows, :] = jnp.where(low, t, 0.0).astype(BF16)
            scr[1, 0, rows, :] = jnp.where(low, 0.0, swapped).astype(BF16)
            scr[0, 1, rows, :] = jnp.where(low, swapped, 0.0).astype(BF16)
            scr[1, 1, rows, :] = jnp.where(low, 0.0, t).astype(BF16)

    w = WINDOW
    qi = lax.broadcasted_iota(jnp.int32, (2 * w, w), 0) % w
    kj = lax.broadcasted_iota(jnp.int32, (2 * w, w), 1)
    top = lax.broadcasted_iota(jnp.int32, (2 * w, 1), 0) < w
    cur_mask = kj <= qi

    def block(n, carry):
        r0 = pl.multiple_of(n * w, w)
        p0 = pl.multiple_of(jnp.maximum(n - 1, 0) * w, w)
        prev_mask = (kj > qi) & (n > 0)
        for g in range(SWA_KV_HEADS):
            qa = jnp.concatenate([q_scr[2 * g, pl.ds(r0, w), :],
                                  q_scr[2 * g + 1, pl.ds(r0, w), :]], axis=0)
            out = jnp.zeros((2 * w, LANES), F32)
            for slot in range(2):
                s_cur = lax.dot_general(qa, k_scr[slot, g, pl.ds(r0, w), :], NT_DIMS,
                                        preferred_element_type=F32)
                s_prev = lax.dot_general(qa, k_scr[slot, g, pl.ds(p0, w), :], NT_DIMS,
                                         preferred_element_type=F32)
                s_cur = jnp.where(cur_mask, s_cur, -jnp.inf)
                s_prev = jnp.where(prev_mask, s_prev, -jnp.inf)
                sink = jnp.where(top, sink_ref[4 * g + slot], sink_ref[4 * g + 2 + slot])
                m = jnp.maximum(jnp.max(jnp.maximum(s_cur, s_prev), axis=-1, keepdims=True), sink)
                p_cur = jnp.exp(s_cur - m)
                p_prev = jnp.exp(s_prev - m)
                den = jnp.sum(p_cur + p_prev, axis=-1, keepdims=True) + jnp.exp(sink - m)
                inv = 1.0 / den
                out = out + jnp.dot((p_cur * inv).astype(BF16), v_scr[slot, g, pl.ds(r0, w), :],
                                    preferred_element_type=F32)
                out = out + jnp.dot((p_prev * inv).astype(BF16), v_scr[slot, g, pl.ds(p0, w), :],
                                    preferred_element_type=F32)
            o_ref[pl.ds(r0, w), 2 * g * LANES:(2 * g + 1) * LANES] = out[:w].astype(BF16)
            o_ref[pl.ds(r0, w), (2 * g + 1) * LANES:(2 * g + 2) * LANES] = out[w:].astype(BF16)
        return carry

    lax.fori_loop(0, seq // w, block, 0, unroll=8)


def _swa_attention(qkv, cos, sin, q_gain, k_gain, sinks, batch, seq):
    width = SWA_Q_W + 2 * SWA_KV_W
    assert AB_IN % width == 0
    const = lambda b: (0, 0)
    return pl.pallas_call(
        functools.partial(_swa_kernel, seq),
        grid=(batch,),
        in_specs=[
            pl.BlockSpec((seq, width), lambda b: (b, 0)),
            pl.BlockSpec((seq, LANES), lambda b: (b, 0)),
            pl.BlockSpec((seq, LANES), lambda b: (b, 0)),
            pl.BlockSpec((1, LANES), const),
            pl.BlockSpec((1, LANES), const),
            pl.BlockSpec(memory_space=pltpu.SMEM),
        ],
        out_specs=pl.BlockSpec((seq, SWA_Q_W), lambda b: (b, 0)),
        out_shape=jax.ShapeDtypeStruct((batch * seq, SWA_Q_W), BF16),
        scratch_shapes=[
            pltpu.VMEM((SWA_Q_W // LANES, seq, LANES), BF16),
            pltpu.VMEM((2, SWA_KV_HEADS, seq, LANES), BF16),
            pltpu.VMEM((2, SWA_KV_HEADS, seq, LANES), BF16),
        ],
        compiler_params=_params("arbitrary"),
    )(qkv, cos, sin,
      jnp.tile(q_gain, LANES // HEAD_DIM).reshape(1, LANES),
      jnp.tile(k_gain, LANES // HEAD_DIM).reshape(1, LANES),
      sinks)


def _neg_abs(x):
    bits = lax.bitcast_convert_type(x, jnp.uint32) | jnp.uint32(0x80000000)
    return lax.bitcast_convert_type(bits, F32)


def _sb_kernel(seq, q_ref, k_ref, v_ref, o_ref, k_scr, v_scr, later_scr, causal_scr, acc_ref,
               carry_ref):
    blk = SB_BLOCK
    tq = min(seq, SB_Q_TILE)
    per_tile = tq // blk
    nb = seq // blk
    low = lax.broadcasted_iota(jnp.int32, (seq, LANES), 1) < HEAD_DIM
    for src, scr in ((k_ref, k_scr), (v_ref, v_scr)):
        full = src[...]
        zero = jnp.zeros_like(full)
        scr[:, 0:blk, :] = jnp.where(low, full, zero).reshape(nb, blk, LANES)
        scr[:, blk:2 * blk, :] = jnp.where(low, zero, full).reshape(nb, blk, LANES)

    r = lax.broadcasted_iota(jnp.int32, (2 * blk, 2 * blk), 0)
    c = lax.broadcasted_iota(jnp.int32, (2 * blk, 2 * blk), 1)
    later_scr[...] = ((r // blk == c // blk) & (r % blk > c % blk)).astype(BF16)
    causal_scr[...] = (lax.broadcasted_iota(jnp.int32, (tq, 2 * blk), 1) % blk
                       < lax.broadcasted_iota(jnp.int32, (tq, 2 * blk), 0)).astype(F32)

    def scores(q_rows, n_rows, kb, mask):
        qt = q_ref[pl.ds(q_rows, n_rows), :] * jnp.asarray(HEAD_DIM ** -0.5, BF16)
        z = lax.dot_general(qt, k_scr[kb], NT_DIMS, preferred_element_type=F32)
        log1p_e = jnp.log(1.0 + jnp.exp(_neg_abs(z)))
        log_beta = jnp.minimum(z, 0.0) - log1p_e
        log_fail = log_beta - z
        if mask is not None:
            log_fail = log_fail * mask
        after = jnp.dot(log_fail.astype(BF16), later_scr[...], preferred_element_type=F32)
        totals = [after[:, h * blk:h * blk + 1] + log_fail[:, h * blk:h * blk + 1]
                  for h in range(2)]
        return log_beta + after, totals

    def weights(parts, carry, mask):
        log_w, totals = parts
        a = jnp.exp(log_w + carry)
        if mask is not None:
            a = a * mask
        halves = [carry[:, h * blk:(h + 1) * blk] + totals[h] for h in range(2)]
        return a.astype(BF16), jnp.concatenate(halves, axis=1)

    def tile(qi, has_below):
        row0 = pl.multiple_of(qi * tq, tq) if has_below else qi * tq
        first = qi * per_tile
        order = list(reversed(range(per_tile)))
        masks = {j: causal_scr[0:tq - j * blk, :] for j in order}
        parts = {j: scores(row0 + j * blk, tq - j * blk, first + j, masks[j]) for j in order}
        group = [first - 1 - u for u in range(SB_GROUP)] if has_below else []
        group_parts = [scores(row0, tq, kb, None) for kb in group]
        carry = jnp.zeros((tq, 2 * blk), F32)
        pv = jnp.zeros((tq, LANES), F32)
        for j in order:
            r0 = j * blk
            a, tail = weights(parts[j], carry[r0:], masks[j])
            upd = pv[r0:] + jnp.dot(a, v_scr[first + j], preferred_element_type=F32)
            carry = tail if r0 == 0 else jnp.concatenate([carry[:r0], tail], axis=0)
            pv = upd if r0 == 0 else jnp.concatenate([pv[:r0], upd], axis=0)
        for kb, p in zip(group, group_parts):
            a, carry = weights(p, carry, None)
            pv = pv + jnp.dot(a, v_scr[kb], preferred_element_type=F32)
        acc_ref[...] = pv
        carry_ref[...] = carry

        def below(state):
            i, _ = state
            kbs = [first - 1 - (SB_GROUP * i + u) for u in range(SB_GROUP)]
            parts = [scores(row0, tq, kb, None) for kb in kbs]
            carry = carry_ref[...]
            pv = jnp.zeros((tq, LANES), F32)
            for kb, p in zip(kbs, parts):
                a, carry = weights(p, carry, None)
                pv = pv + jnp.dot(a, v_scr[kb], preferred_element_type=F32)
            acc_ref[...] += pv
            carry_ref[...] = carry
            return i + 1, jnp.max(carry)

        if has_below:
            n_groups = qi * (per_tile // SB_GROUP)
            lax.while_loop(lambda s: (s[0] < n_groups) & (s[1] > SB_EXP_ZERO), below,
                           (jnp.int32(1), jnp.max(carry)))
        o_ref[pl.ds(row0, tq), :] = acc_ref[...].astype(BF16)

    tile(0, False)

    def later_tile(qi, c_):
        tile(qi, True)
        return c_

    lax.fori_loop(1, seq // tq, later_tile, 0)


def _sb_attention(qkv, batch, seq):
    tq = min(seq, SB_Q_TILE)
    pairs = SB_W // LANES
    q0 = (SWA_Q_W + 2 * SWA_KV_W) // LANES
    k0, v0 = q0 + pairs, q0 + 2 * pairs
    return pl.pallas_call(
        functools.partial(_sb_kernel, seq),
        grid=(batch, pairs),
        in_specs=[
            pl.BlockSpec((seq, LANES), lambda b, p: (b, q0 + p)),
            pl.BlockSpec((seq, LANES), lambda b, p: (b, k0 + p)),
            pl.BlockSpec((seq, LANES), lambda b, p: (b, v0 + p)),
        ],
        out_specs=pl.BlockSpec((seq, LANES), lambda b, p: (b, p)),
        out_shape=jax.ShapeDtypeStruct((batch * seq, SB_W), BF16),
        scratch_shapes=[
            pltpu.VMEM((seq // SB_BLOCK, 2 * SB_BLOCK, LANES), BF16),
            pltpu.VMEM((seq // SB_BLOCK, 2 * SB_BLOCK, LANES), BF16),
            pltpu.VMEM((2 * SB_BLOCK, 2 * SB_BLOCK), BF16),
            pltpu.VMEM((tq, 2 * SB_BLOCK), F32),
            pltpu.VMEM((tq, LANES), F32),
            pltpu.VMEM((tq, 2 * SB_BLOCK), F32),
        ],
        compiler_params=_params("arbitrary", "arbitrary"),
    )(qkv, qkv, qkv)


def _hgrn_kernel(seq, q_ref, hi_ref, i_ref, g_ref, lo_ref, k_ref, on_ref, o_ref,
                 qs_scr, ks_scr, b_scr, o_scr, dec_scr, st_scr, qrow_scr):
    ch = HGRN_CHUNK
    sc = HGRN_STATE_CHUNK
    hd = HGRN_HEAD_DIM
    heads = HGRN_HEADS_PER_STEP
    width = heads * hd
    pt = min(seq, HGRN_PRE_TILE)

    r = lax.broadcasted_iota(jnp.int32, (pt, pt), 0)
    c = lax.broadcasted_iota(jnp.int32, (pt, pt), 1)
    same_sub = (r // ch) == (c // ch)
    same_chunk = (r // sc) == (c // sc)
    m_diag = same_sub & (c <= r)
    m_prev = same_chunk & ((r // ch) == (c // ch) + 1)
    twice = lambda m: jnp.concatenate([m.astype(BF16)] * 2, axis=1)
    cum_sub, tot_sub, tot_chunk = twice(m_diag), twice(same_sub), twice(same_chunk)
    first_sub = (lax.broadcasted_iota(jnp.int32, (pt, 1), 0) // ch) % (sc // ch) == 0
    s_idx = lax.broadcasted_iota(jnp.int32, (ch, 1), 0)
    row8 = lax.broadcasted_iota(jnp.int32, (8, 1), 0)

    def exact_diagonal(row0):
        def sub_chunk(n, carry):
            r0 = pl.multiple_of(row0 + n * ch, ch)
            for h in range(heads):
                cols = slice(h * hd, (h + 1) * hd)
                b_c = b_scr[pl.ds(r0, ch), cols]
                k_c = k_ref[pl.ds(r0, ch), cols].astype(F32)
                v_c = i_ref[pl.ds(r0, ch), cols].astype(F32)
                qrow_scr[...] = q_ref[pl.ds(r0, ch), cols].astype(F32)

                def row(t, cc, b_c=b_c, k_c=k_c, v_c=v_c, cols=cols):
                    g0 = pl.multiple_of((t // 8) * 8, 8)
                    pick = row8 == (t % 8)
                    take = lambda blk: jnp.sum(jnp.where(pick, blk, 0.0), axis=0, keepdims=True)
                    b_t = take(b_scr[pl.ds(r0 + g0, 8), cols])
                    q_t = take(qrow_scr[pl.ds(g0, 8), :])
                    w = jnp.where(s_idx <= t, jnp.exp(jnp.minimum(b_t - b_c, 0.0)), 0.0)
                    sc_t = jnp.sum(q_t * k_c * w, axis=-1, keepdims=True)
                    o_t = jnp.sum(sc_t * v_c, axis=0, keepdims=True)
                    o_scr[pl.ds(r0 + g0, 8), cols] += jnp.where(pick, o_t, 0.0)
                    return cc

                lax.fori_loop(0, ch, row, 0)
            return carry

        lax.fori_loop(0, pt // ch, sub_chunk, 0)

    def gates(t):
        rows = slice(t * pt, (t + 1) * pt)
        return (k_ref[rows, :].astype(F32),
                jnp.concatenate([hi_ref[rows, :], lo_ref[rows, :]], axis=0))

    def decays(parts):
        return [jnp.dot(m, parts, preferred_element_type=F32)
                for m in (cum_sub, tot_sub, tot_chunk)]

    def factors(t, kk, b, end_sub, end_chunk):
        rows = slice(t * pt, (t + 1) * pt)
        before = jnp.where(first_sub, 0.0, end_chunk - end_sub)
        qd = q_ref[rows, :].astype(F32) * jnp.exp(b)
        qs_scr[rows, :] = (qd * jnp.exp(before)).astype(BF16)
        ks_scr[rows, :] = (kk * jnp.exp(end_chunk - b - before)).astype(BF16)
        b_scr[rows, :] = b
        for i in range(pt // sc):
            dec_scr[t * (pt // sc) + i] = jnp.broadcast_to(
                jnp.exp(end_chunk[i * sc:i * sc + 1, :]), (8, width))
        safe = jnp.min(end_sub) > HGRN_SAFE_LOG_DECAY
        return (qd.astype(BF16), (kk * jnp.exp(-b)).astype(BF16),
                (kk * jnp.exp(end_sub - b)).astype(BF16), safe)

    def intra(t, qd_b, kd_b, ke_b, safe):
        rows = slice(t * pt, (t + 1) * pt)
        v = i_ref[rows, :]
        for h in range(heads):
            cols = slice(h * hd, (h + 1) * hd)
            a_diag = lax.dot_general(qd_b[:, cols], kd_b[:, cols], NT_DIMS,
                                     preferred_element_type=F32)
            a_prev = lax.dot_general(qd_b[:, cols], ke_b[:, cols], NT_DIMS,
                                     preferred_element_type=F32)
            a = jnp.where(m_diag, jnp.where(safe, a_diag, 0.0), jnp.where(m_prev, a_prev, 0.0))
            o_scr[rows, cols] = jnp.dot(a.astype(BF16), v[:, cols], preferred_element_type=F32)

    n_tiles = seq // pt
    g_out, d_out, f_out = {}, {}, {}
    for step in range(n_tiles + 3):
        if step < n_tiles:
            g_out[step] = gates(step)
        if 0 <= step - 1 < n_tiles:
            d_out[step - 1] = decays(g_out[step - 1][1])
        if 0 <= step - 2 < n_tiles:
            f_out[step - 2] = factors(step - 2, g_out[step - 2][0], *d_out[step - 2])
        if 0 <= step - 3 < n_tiles:
            intra(step - 3, *f_out[step - 3])

    for t in range(n_tiles):
        @pl.when(jnp.logical_not(f_out[t][3]))
        def _(t=t):
            exact_diagonal(t * pt)

    st_scr[...] = jnp.zeros_like(st_scr)

    def state_step(n, carry):
        r0 = pl.multiple_of(n * sc, sc)
        decay = dec_scr[n]
        for h in range(heads):
            cols = slice(h * hd, (h + 1) * hd)
            st = st_scr[h]
            o_scr[pl.ds(r0, sc), cols] += lax.dot_general(
                qs_scr[pl.ds(r0, sc), cols], st.astype(BF16), NT_DIMS, preferred_element_type=F32)
            kv = lax.dot_general(i_ref[pl.ds(r0, sc), cols], ks_scr[pl.ds(r0, sc), cols], TN_DIMS,
                                 preferred_element_type=F32)
            st_scr[h] = st * decay[0:1, cols] + kv
        return carry

    lax.fori_loop(0, seq // sc, state_step, 0, unroll=8)

    for t in range(seq // pt):
        rows = slice(t * pt, (t + 1) * pt)
        silu = g_ref[rows, :].astype(F32)
        for h in range(heads):
            cols = slice(h * hd, (h + 1) * hd)
            y = _rms(o_scr[rows, cols], on_ref[...])
            o_ref[rows, cols] = (y * silu[:, cols]).astype(BF16)


def _hgrn_mixer_core(proj, o_gain, batch, seq):
    heads = HGRN_HEADS_PER_STEP
    width = heads * HGRN_HEAD_DIM
    groups = HGRN_HEADS // heads
    assert seq % HGRN_PRE_TILE == 0 and HGRN_PRE_TILE % HGRN_STATE_CHUNK == 0
    section = lambda k: pl.BlockSpec((seq, width), lambda b, g, k=k: (b, k * groups + g))
    return pl.pallas_call(
        functools.partial(_hgrn_kernel, seq),
        grid=(batch, groups),
        in_specs=[section(k) for k in range(6)]
        + [pl.BlockSpec((1, HGRN_HEAD_DIM), lambda b, g: (0, 0))],
        out_specs=pl.BlockSpec((seq, width), lambda b, g: (b, g)),
        out_shape=jax.ShapeDtypeStruct((batch * seq, D_MODEL), BF16),
        scratch_shapes=[
            pltpu.VMEM((seq, width), BF16),
            pltpu.VMEM((seq, width), BF16),
            pltpu.VMEM((seq, width), F32),
            pltpu.VMEM((seq, width), F32),
            pltpu.VMEM((seq // HGRN_STATE_CHUNK, 8, width), F32),
            pltpu.VMEM((heads, HGRN_HEAD_DIM, HGRN_HEAD_DIM), F32),
            pltpu.VMEM((HGRN_CHUNK, HGRN_HEAD_DIM), F32),
        ],
        compiler_params=_params("arbitrary", "arbitrary"),
    )(*([proj] * 6), o_gain.reshape(1, HGRN_HEAD_DIM))


def _xa_kv_kernel(mem_ref, g_ref, w_ref, kn_ref, k_ref, v_ref):
    mem_n = _rms(mem_ref[...], g_ref[...]).astype(BF16)
    kv = jnp.dot(mem_n, w_ref[...], preferred_element_type=F32)
    for h in range(XA_HEADS):
        cols = slice(h * XA_HEAD_DIM, (h + 1) * XA_HEAD_DIM)
        k_ref[:, cols] = _rms(kv[:, cols], kn_ref[...]).astype(BF16)
    v_ref[...] = kv[:, D_MODEL:].astype(BF16)


def _xa_kv(mem, gain, w_kv, k_gain):
    m, d = mem.shape
    tm = min(m, TOKEN_TILE)
    row = lambda i: (i, 0)
    return pl.pallas_call(
        _xa_kv_kernel,
        grid=(m // tm,),
        in_specs=[pl.BlockSpec((tm, d), row), pl.BlockSpec((1, d), lambda i: (0, 0)),
                  _resident(w_kv.shape), pl.BlockSpec((1, XA_HEAD_DIM), lambda i: (0, 0))],
        out_specs=[pl.BlockSpec((tm, d), row), pl.BlockSpec((tm, d), row)],
        out_shape=[jax.ShapeDtypeStruct((m, d), BF16)] * 2,
        compiler_params=_params("arbitrary"),
    )(mem, gain.reshape(1, d), w_kv, k_gain.reshape(1, XA_HEAD_DIM))


def _xa_kernel(n_in, x_ref, *refs):
    a_refs, w_refs = refs[:n_in], refs[n_in:2 * n_in]
    g_ref, wq_ref, qn_ref, k_ref, v_ref, wo_ref, o_ref = refs[2 * n_in:]
    x = x_ref[...]
    for a_ref, w_ref in zip(a_refs, w_refs):
        x = x + jnp.dot(a_ref[...], w_ref[...], preferred_element_type=F32)
    h = _rms(x, g_ref[...]).astype(BF16)
    q = jnp.dot(h, wq_ref[...], preferred_element_type=F32)
    heads = []
    for hd in range(XA_HEADS):
        cols = slice(hd * XA_HEAD_DIM, (hd + 1) * XA_HEAD_DIM)
        qh = (_rms(q[:, cols], qn_ref[...]) * XA_HEAD_DIM ** -0.5).astype(BF16)
        s = lax.dot_general(qh, k_ref[:, cols], NT_DIMS, preferred_element_type=F32)
        p = jnp.exp(s - jnp.max(s, axis=-1, keepdims=True))
        p = p * (1.0 / jnp.sum(p, axis=-1, keepdims=True))
        heads.append(jnp.dot(p.astype(BF16), v_ref[:, cols],
                             preferred_element_type=F32).astype(BF16))
    o = jnp.concatenate(heads, axis=1)
    o_ref[...] = x + jnp.dot(o, wo_ref[...], preferred_element_type=F32)


def _cross_attention(x, acts, weights, gain, w_q, q_gain, k, v, w_o, seq, n_mem):
    m, d = x.shape
    tm = min(seq, TOKEN_TILE)
    per_seq = seq // tm
    row = lambda i: (i, 0)
    mem_row = lambda i: (i // per_seq, 0)
    return pl.pallas_call(
        functools.partial(_xa_kernel, len(acts)),
        grid=(m // tm,),
        in_specs=[pl.BlockSpec((tm, d), row)]
        + [pl.BlockSpec((tm, a.shape[1]), row) for a in acts]
        + [_resident(w.shape) for w in weights]
        + [pl.BlockSpec((1, d), lambda i: (0, 0)),
           _resident(w_q.shape), pl.BlockSpec((1, XA_HEAD_DIM), lambda i: (0, 0)),
           pl.BlockSpec((n_mem, d), mem_row), pl.BlockSpec((n_mem, d), mem_row),
           _resident(w_o.shape)],
        out_specs=pl.BlockSpec((tm, d), row),
        out_shape=jax.ShapeDtypeStruct((m, d), F32),
        compiler_params=_params("arbitrary"),
    )(x, *acts, *weights, gain.reshape(1, d), w_q, q_gain.reshape(1, XA_HEAD_DIM), k, v, w_o)


def _ffn_kernel(per_seq, d_ff, x_ref, g_ref, wup_ref, cw_ref, cb_ref, wdn_ref, o_ref,
                halo_ref):
    tm = x_ref.shape[0]
    fc = FF_CHUNK
    n_chunks = d_ff // fc
    groups = tm // 8

    @pl.when(pl.program_id(0) % per_seq == 0)
    def _():
        halo_ref[...] = jnp.zeros_like(halo_ref)

    x = x_ref[...]
    h = _rms(x, g_ref[...]).astype(BF16)
    sub = lax.broadcasted_iota(jnp.int32, (groups, 8, fc), 1)


    def up_proj(j):
        return [jnp.dot(h, wup_ref[:, c0:c0 + fc], preferred_element_type=F32)
                for c0 in (j * fc, d_ff + j * fc)]

    def conv(u, col0, slot):
        cols = slice(col0, col0 + fc)
        prev = halo_ref[slot]
        halo_ref[slot] = u[tm - 8:, :]
        grouped = jnp.concatenate([prev, u], axis=0).reshape(groups + 1, 8, fc)
        out = cw_ref[2:3, cols] * u + cb_ref[:, cols]
        for n in (1, 2):
            rot = pltpu.roll(grouped, n, 1)
            shifted = jnp.where(sub < n, rot[:groups], rot[1:]).reshape(tm, fc)
            out = out + cw_ref[2 - n:3 - n, cols] * shifted
        return out

    nxt = up_proj(0)
    acts = []
    for j in range(n_chunks):
        cur = nxt
        if j + 1 < n_chunks:
            nxt = up_proj(j + 1)
        gate = conv(cur[0], j * fc, j)
        up = conv(cur[1], d_ff + j * fc, n_chunks + j)
        acts.append((gate * (1.0 / (1.0 + jnp.exp(-gate))) * up).astype(BF16))
    o_ref[...] = x + jnp.dot(jnp.concatenate(acts, axis=1), wdn_ref[...],
                             preferred_element_type=F32)


def _conv_ffn(x, gain, w_up, conv_w, conv_b, w_down, seq):
    m, d = x.shape
    d_ff = w_down.shape[0]
    assert d_ff % FF_CHUNK == 0
    tm = min(seq, TOKEN_TILE)
    row = lambda i: (i, 0)
    const = lambda i: (0, 0)
    return pl.pallas_call(
        functools.partial(_ffn_kernel, seq // tm, d_ff),
        grid=(m // tm,),
        in_specs=[pl.BlockSpec((tm, d), row), pl.BlockSpec((1, d), const),
                  _resident(w_up.shape), pl.BlockSpec(conv_w.shape, const),
                  pl.BlockSpec((1, 2 * d_ff), const), _resident(w_down.shape)],
        out_specs=pl.BlockSpec((tm, d), row),
        out_shape=jax.ShapeDtypeStruct((m, d), F32),
        scratch_shapes=[pltpu.VMEM((2 * d_ff // FF_CHUNK, 8, FF_CHUNK), F32)],
        compiler_params=_params("arbitrary"),
    )(x, gain.reshape(1, d), w_up, conv_w, conv_b.reshape(1, 2 * d_ff), w_down)


def kernel(x, mem, positions, norm_mix, norm_cross, norm_mem, norm_ffn, ab_w_in, ab_w_out, swa_q_norm, swa_k_norm, swa_sinks, hgrn_w_in, hgrn_w_out, hgrn_o_norm, hgrn_lb, xa_w_q, xa_w_kv, xa_w_o, xa_q_norm, xa_k_norm, ffn_w_up, ffn_conv_w, ffn_conv_b, ffn_w_down):
    batch, seq, d = x.shape
    n_mem = mem.shape[1]
    depth = norm_mix.shape[0]
    assert d == D_MODEL and seq % TOKEN_TILE == 0
    assert (batch * n_mem) % min(batch * n_mem, TOKEN_TILE) == 0
    bf = lambda w: w.astype(BF16)

    p_lb = jax.nn.softmax(hgrn_lb.astype(F32), axis=0)
    lower_bounds = jnp.cumsum(p_lb, axis=0) - p_lb[0]

    xf = x.reshape(batch * seq, d)
    memf = mem.reshape(batch * n_mem, d)
    cos, sin = _rope_tables(positions)

    for l in range(depth):
        if l % 2 == 0:
            e = l // 2
            qkv = _norm_matmul(xf, norm_mix[l], bf(ab_w_in[e]), tn=AB_IN // 3)
            out_a = _swa_attention(qkv, cos, sin, swa_q_norm[e], swa_k_norm[e], swa_sinks[e],
                                   batch, seq)
            out_b = _sb_attention(qkv, batch, seq)
            w_out = bf(ab_w_out[e])
            mixed, mix_w = [out_a, out_b], [w_out[:SWA_Q_W], w_out[SWA_Q_W:]]
        else:
            o = l // 2
            proj = _hgrn_proj(xf, norm_mix[l], bf(hgrn_w_in[o]), lower_bounds[l])
            mixed = [_hgrn_mixer_core(proj, hgrn_o_norm[o], batch, seq)]
            mix_w = [bf(hgrn_w_out[o])]
        k, v = _xa_kv(memf, norm_mem[l], bf(xa_w_kv[l]), xa_k_norm[l])
        xf = _cross_attention(xf, mixed, mix_w, norm_cross[l], bf(xa_w_q[l]), xa_q_norm[l], k, v,
                              bf(xa_w_o[l]), seq, n_mem)
        xf = _conv_ffn(xf, norm_ffn[l], bf(ffn_w_up[l]), ffn_conv_w[l], ffn_conv_b[l],
                       bf(ffn_w_down[l]), seq)
    return xf.reshape(batch, seq, d)
```
